```python
import math
import jax, jax.numpy as jnp
from jax import lax
import numpy as np

D_MODEL = 2048
BATCH = 2
SEQ = 4096
DEPTH = 1

D_MIX = D_MODEL
MLSTM_WIDTH = D_MIX // 2
MLSTM_HEADS = 4
MLSTM_HEAD_DIM = MLSTM_WIDTH // MLSTM_HEADS
MLSTM_CHUNK = 128
CONV_WIDTH = 4
S5_WIDTH = D_MIX - MLSTM_WIDTH
S5_GROUP = 16
S5_GROUPS = S5_WIDTH // S5_GROUP
S5_STATE = 64
S5_DT_MIN = 1e-3
S5_DT_MAX = 1e-1
PEER_HEADS = 8
PEER_NKEYS = 128
PEER_EXPERTS = PEER_NKEYS * PEER_NKEYS
PEER_TOPK = 16
PEER_QDIM = 256
PEER_BLOCK = 128
RMS_EPS = 1e-6
N_IN = 4 * MLSTM_WIDTH + 2 * MLSTM_HEADS + S5_WIDTH

kernel_name = 'hybrid_mlstm_s5_peer_block'


def rmsnorm(x, g):
    xf = x.astype(jnp.float32)
    y = xf * lax.rsqrt(jnp.mean(xf * xf, axis=-1, keepdims=True) + RMS_EPS)
    return (y * g.astype(jnp.float32)).astype(x.dtype)


def causal_dwconv(x, w):
    K, C = w.shape
    return lax.conv_general_dilated(
        x, w[:, None, :].astype(x.dtype), window_strides=(1,), padding=((K - 1, 0),),
        dimension_numbers=('NWC', 'WIO', 'NWC'), feature_group_count=C)


def mlstm_chunkwise(q, k, v, li, lf):
    B, H, S, Dh = q.shape
    L = MLSTM_CHUNK
    NC = S // L
    q = q.reshape(B, H, NC, L, Dh) * (Dh ** -0.5)
    k = k.reshape(B, H, NC, L, Dh)
    v = v.reshape(B, H, NC, L, Dh)
    li = li.reshape(B, H, NC, L)
    lf = lf.reshape(B, H, NC, L)
    a = jnp.cumsum(lf, axis=-1)
    A = a[..., -1]
    g = A[..., None] - a + li
    m_loc = jnp.max(g, axis=-1)
    w = jnp.exp(g - m_loc[..., None])
    dC = jnp.einsum('bhcl,bhcld,bhcle->bhcde', w, k, v)
    dn = jnp.einsum('bhcl,bhcld->bhcd', w, k)

    def step(carry, inp):
        C, n, m = carry
        dC_c, dn_c, m_c, A_c = inp
        m_new = jnp.maximum(A_c + m, m_c)
        s_old = jnp.exp(A_c + m - m_new)
        s_new = jnp.exp(m_c - m_new)
        C_new = s_old[..., None, None] * C + s_new[..., None, None] * dC_c
        n_new = s_old[..., None] * n + s_new[..., None] * dn_c
        return (C_new, n_new, m_new), (C, n, m)

    init = (jnp.zeros((B, H, Dh, Dh), jnp.float32), jnp.zeros((B, H, Dh), jnp.float32),
            jnp.zeros((B, H), jnp.float32))
    xs = (jnp.moveaxis(dC, 2, 0), jnp.moveaxis(dn, 2, 0), jnp.moveaxis(m_loc, 2, 0), jnp.moveaxis(A, 2, 0))
    _, (C0, n0, m0) = lax.scan(step, init, xs)
    C0 = jnp.moveaxis(C0, 0, 2)
    n0 = jnp.moveaxis(n0, 0, 2)
    m0 = jnp.moveaxis(m0, 0, 2)

    Dmat = a[..., :, None] - a[..., None, :] + li[..., None, :]
    causal = jnp.tril(jnp.ones((L, L), dtype=bool))
    Dmat = jnp.where(causal, Dmat, -jnp.inf)
    m_inter = a + m0[..., None]
    m_t = jnp.maximum(m_inter, jnp.max(Dmat, axis=-1))
    Wm = jnp.exp(Dmat - m_t[..., None])
    s_inter = jnp.exp(m_inter - m_t)
    qk = jnp.einsum('bhcjd,bhcsd->bhcjs', q, k) * Wm
    num = jnp.einsum('bhcjs,bhcse->bhcje', qk, v) + s_inter[..., None] * jnp.einsum('bhcjd,bhcde->bhcje', q, C0)
    den = jnp.sum(qk, axis=-1) + s_inter * jnp.einsum('bhcjd,bhcd->bhcj', q, n0)
    h = num / jnp.maximum(jnp.abs(den), jnp.exp(-m_t))[..., None]
    return h.reshape(B, H, S, Dh)


def s5_groups(u, lam_re, lam_im, log_dt, b_re, b_im, c_re, c_im, d, glu_w):
    S = u.shape[1]
    f32 = jnp.float32
    lam_re = lam_re.astype(f32); lam_im = lam_im.astype(f32)
    dt = jnp.exp(log_dt.astype(f32))[:, None]
    mag = jnp.exp(lam_re * dt)
    ar = mag * jnp.cos(lam_im * dt)
    ai = mag * jnp.sin(lam_im * dt)
    den = lam_re * lam_re + lam_im * lam_im
    fr = ((ar - 1.0) * lam_re + ai * lam_im) / den
    fi = (ai * lam_re - (ar - 1.0) * lam_im) / den
    b_re = b_re.astype(f32); b_im = b_im.astype(f32)
    bbr = fr[..., None] * b_re - fi[..., None] * b_im
    bbi = fr[..., None] * b_im + fi[..., None] * b_re
    xr = jnp.einsum('bsgh,gph->bsgp', u, bbr)
    xi = jnp.einsum('bsgh,gph->bsgp', u, bbi)
    G, P = ar.shape
    a_r = jnp.broadcast_to(ar[None, None], (1, S, G, P))
    a_i = jnp.broadcast_to(ai[None, None], (1, S, G, P))

    def combine(e1, e2):
        a1r, a1i, b1r, b1i = e1
        a2r, a2i, b2r, b2i = e2
        return (a1r * a2r - a1i * a2i, a1r * a2i + a1i * a2r,
                a2r * b1r - a2i * b1i + b2r, a2r * b1i + a2i * b1r + b2i)

    _, _, sr, si = lax.associative_scan(combine, (a_r, a_i, xr, xi), axis=1)
    y = (jnp.einsum('bsgp,ghp->bsgh', sr, c_re.astype(f32))
         - jnp.einsum('bsgp,ghp->bsgh', si, c_im.astype(f32))
         + d.astype(f32) * u)
    ab = jnp.einsum('bsgh,ghk->bsgk', jax.nn.gelu(y, approximate=False), glu_w.astype(f32))
    return ab[..., :S5_GROUP] * jax.nn.sigmoid(ab[..., S5_GROUP:])


def peer(x, wq, subkeys, u_tab, v_tab):
    B, S, D = x.shape
    f32 = jnp.float32
    half = PEER_QDIM // 2
    q = (x @ wq).astype(f32).reshape(B, S, PEER_HEADS, 2, half)
    s = jnp.einsum('bshpd,pkd->bshpk', q, subkeys.astype(f32))
    s1, i1 = lax.top_k(s[..., 0, :], PEER_TOPK)
    s2, i2 = lax.top_k(s[..., 1, :], PEER_TOPK)
    cand_s = (s1[..., :, None] + s2[..., None, :]).reshape(B, S, PEER_HEADS, PEER_TOPK * PEER_TOPK)
    cand_i = (i1[..., :, None] * PEER_NKEYS + i2[..., None, :]).reshape(B, S, PEER_HEADS, PEER_TOPK * PEER_TOPK)
    top_s, pos = lax.top_k(cand_s, PEER_TOPK)
    idx = jnp.take_along_axis(cand_i, pos, axis=-1)
    gate = jax.nn.softmax(top_s, axis=-1)
    T = B * S
    HK = PEER_HEADS * PEER_TOPK
    xb = x.reshape(T // PEER_BLOCK, PEER_BLOCK, D)
    ib = idx.reshape(T // PEER_BLOCK, PEER_BLOCK, HK)
    gb = gate.reshape(T // PEER_BLOCK, PEER_BLOCK, HK)

    def block(args):
        xt, it, gt = args
        ue = jnp.take(u_tab, it, axis=0)
        ve = jnp.take(v_tab, it, axis=0)
        act = jax.nn.gelu(jnp.einsum('td,ted->te', xt, ue).astype(f32), approximate=False)
        return jnp.einsum('te,ted->td', (gt * act).astype(ve.dtype), ve)

    out = lax.map(block, (xb, ib, gb))
    return out.reshape(B, S, D)


def setup_inputs(seed: int = 0) -> dict:
    key = jax.random.key(seed)
    ks = jax.random.split(key, 24)
    f32 = jnp.float32
    nrm = lambda k, shape, s: jax.random.normal(k, shape, f32) * s
    L = DEPTH
    G, P, Hg = S5_GROUPS, S5_STATE, S5_GROUP
    f_bias = jnp.broadcast_to(jnp.linspace(3.0, 6.0, MLSTM_HEADS, dtype=f32), (L, MLSTM_HEADS))
    b_gates = jnp.concatenate([nrm(ks[3], (L, MLSTM_HEADS), 0.1),
                               f_bias + nrm(ks[4], (L, MLSTM_HEADS), 0.1)], axis=-1)
    lam_im0 = jnp.pi * jnp.arange(P, dtype=f32)
    log_dt = jax.random.uniform(ks[8], (L, G), f32, math.log(S5_DT_MIN), math.log(S5_DT_MAX))
    return dict(
        x=nrm(ks[0], (BATCH, SEQ, D_MODEL), 1.0),
        norm1_g=1.0 + nrm(ks[1], (L, D_MODEL), 0.1),
        w_in=nrm(ks[2], (L, D_MODEL, N_IN), D_MODEL ** -0.5),
        b_gates=b_gates,
        conv_qk_w=nrm(ks[5], (L, CONV_WIDTH, 2 * MLSTM_WIDTH), CONV_WIDTH ** -0.5),
        mlstm_norm_g=1.0 + nrm(ks[6], (L, MLSTM_WIDTH), 0.1),
        s5_lambda_re=-0.5 + nrm(ks[7], (L, G, P), 0.01),
        s5_lambda_im=lam_im0 + nrm(ks[9], (L, G, P), 0.01),
        s5_log_dt=log_dt,
        s5_b_re=nrm(ks[10], (L, G, P, Hg), (2 * Hg) ** -0.5),
        s5_b_im=nrm(ks[11], (L, G, P, Hg), (2 * Hg) ** -0.5),
        s5_c_re=nrm(ks[12], (L, G, Hg, P), P ** -0.5),
        s5_c_im=nrm(ks[13], (L, G, Hg, P), P ** -0.5),
        s5_d=nrm(ks[14], (L, G, Hg), 0.5),
        s5_glu_w=nrm(ks[15], (L, G, Hg, 2 * Hg), Hg ** -0.5),
        w_out=nrm(ks[16], (L, D_MIX, D_MODEL), D_MIX ** -0.5),
        norm2_g=1.0 + nrm(ks[17], (L, D_MODEL), 0.1),
        peer_wq=nrm(ks[18], (L, D_MODEL, PEER_HEADS * PEER_QDIM), D_MODEL ** -0.5),
        peer_subkeys=nrm(ks[19], (L, 2, PEER_NKEYS, PEER_QDIM // 2), (PEER_QDIM // 2) ** -0.5),
        peer_u=nrm(ks[20], (L, PEER_EXPERTS, D_MODEL), D_MODEL ** -0.5),
        peer_v=nrm(ks[21], (L, PEER_EXPERTS, D_MODEL), PEER_HEADS ** -0.5),
        final_g=1.0 + nrm(ks[22], (D_MODEL,), 0.1),
    )


def reference(x, norm1_g, w_in, b_gates, conv_qk_w, mlstm_norm_g, s5_lambda_re, s5_lambda_im,
              s5_log_dt, s5_b_re, s5_b_im, s5_c_re, s5_c_im, s5_d, s5_glu_w, w_out, norm2_g,
              peer_wq, peer_subkeys, peer_u, peer_v, final_g):
    B, S, _ = x.shape
    f32 = jnp.float32
    W, H, Dh = MLSTM_WIDTH, MLSTM_HEADS, MLSTM_HEAD_DIM
    h = x
    for l in range(DEPTH):
        xn = rmsnorm(h, norm1_g[l])
        z = xn @ w_in[l]
        q, k, v, o, gates, u = jnp.split(z, [W, 2 * W, 3 * W, 4 * W, 4 * W + 2 * H], axis=-1)
        qk = jax.nn.silu(causal_dwconv(jnp.concatenate([q, k], axis=-1), conv_qk_w[l]))
        q, k = qk[..., :W], qk[..., W:]
        to_heads = lambda t: t.astype(f32).reshape(B, S, H, Dh).transpose(0, 2, 1, 3)
        gates = gates.astype(f32) + b_gates[l].astype(f32)
        li = gates[..., :H].transpose(0, 2, 1)
        lf = jax.nn.log_sigmoid(gates[..., H:]).transpose(0, 2, 1)
        hm = mlstm_chunkwise(to_heads(q), to_heads(k), to_heads(v), li, lf)
        hm = hm.transpose(0, 2, 1, 3)
        hm = hm * lax.rsqrt(jnp.mean(hm * hm, axis=-1, keepdims=True) + RMS_EPS)
        hm = hm.reshape(B, S, W) * mlstm_norm_g[l].astype(f32) * jax.nn.sigmoid(o.astype(f32))
        ys = s5_groups(u.astype(f32).reshape(B, S, S5_GROUPS, S5_GROUP),
                       s5_lambda_re[l], s5_lambda_im[l], s5_log_dt[l], s5_b_re[l], s5_b_im[l],
                       s5_c_re[l], s5_c_im[l], s5_d[l], s5_glu_w[l]).reshape(B, S, S5_WIDTH)
        mix = jnp.concatenate([hm, ys], axis=-1).astype(h.dtype) @ w_out[l]
        h = h + mix.astype(h.dtype)
        h = h + peer(rmsnorm(h, norm2_g[l]), peer_wq[l], peer_subkeys[l], peer_u[l], peer_v[l]).astype(h.dtype)
    return rmsnorm(h, final_g)
```

```python
import functools
import math

import jax
import jax.numpy as jnp
import numpy as np
from jax import lax
from jax.experimental import pallas as pl
from jax.experimental.pallas import tpu as pltpu

F32 = jnp.float32
BF16 = jnp.bfloat16

D_MODEL = 2048
MLSTM_WIDTH = 1024
MLSTM_HEADS = 4
HEAD_DIM = 256
CHUNK = 128
CONV_WIDTH = 4
S5_WIDTH = 1024
S5_GROUP = 16
S5_GROUPS = 64
S5_STATE = 64
PEER_HEADS = 8
PEER_NKEYS = 128
PEER_EXPERTS = PEER_NKEYS * PEER_NKEYS
PEER_TOPK = 16
PEER_QDIM = 256
RMS_EPS = 1e-6

LANES = 128
SUBLANES = 8
VMEM_LIMIT = 56 * 1024 * 1024

NT_DIMS = (((1,), (1,)), ((), ()))


def _params(*sem):
    return pltpu.CompilerParams(dimension_semantics=sem, vmem_limit_bytes=VMEM_LIMIT)


def _rms_scale(v):
    return lax.rsqrt(jnp.mean(v * v, axis=-1, keepdims=True) + RMS_EPS)


def _gelu(v):
    return 0.5 * v * (1.0 + lax.erf(v * np.float32(math.sqrt(0.5))))


def _sigmoid(v):
    return 1.0 / (1.0 + jnp.exp(-v))


def _log_sigmoid(v):
    return jnp.minimum(v, 0.0) - jnp.log1p(jnp.exp(-jnp.abs(v)))


def _inproj_kernel(x_ref, g_ref, w_ref, wg_ref, z_ref, gates_ref, xn_ref):
    @pl.when(pl.program_id(1) == 0)
    def _():
        x = x_ref[...]
        xn_ref[...] = (x * _rms_scale(x) * g_ref[...]).astype(BF16)
        gates_ref[...] = jnp.dot(xn_ref[...], wg_ref[...], preferred_element_type=F32)

    z_ref[...] = jnp.dot(xn_ref[...], w_ref[...], preferred_element_type=F32)


def _inproj(x2d, g_row, w_main, w_gates, tm=1024, tn=1024):
    T, D = x2d.shape
    N = w_main.shape[1]
    return pl.pallas_call(
        _inproj_kernel,
        grid=(T // tm, N // tn),
        in_specs=[
            pl.BlockSpec((tm, D), lambda i, j: (i, 0)),
            pl.BlockSpec((1, D), lambda i, j: (0, 0)),
            pl.BlockSpec((D, tn), lambda i, j: (0, j)),
            pl.BlockSpec((D, LANES), lambda i, j: (0, 0)),
        ],
        out_specs=[
            pl.BlockSpec((tm, tn), lambda i, j: (i, j)),
            pl.BlockSpec((tm, LANES), lambda i, j: (i, 0)),
        ],
        out_shape=[jax.ShapeDtypeStruct((T, N), F32), jax.ShapeDtypeStruct((T, LANES), F32)],
        scratch_shapes=[pltpu.VMEM((tm, D), BF16)],
        compiler_params=_params("parallel", "arbitrary"),
        name="inproj",
    )(x2d, g_row, w_main, w_gates)


def _mlstm_kernel(q_ref, k_ref, v_ref, o_ref, gates_ref, bias_ref, convw_ref, ng_ref, hm_ref,
                  xe_ref, c_ref, n_ref, m_ref):
    L, H, Dh, W = CHUNK, MLSTM_HEADS, HEAD_DIM, MLSTM_WIDTH
    tail = SUBLANES

    @pl.when(pl.program_id(1) == 0)
    def _():
        xe_ref[0:tail, :] = jnp.zeros((tail, 2 * W), F32)
        c_ref[...] = jnp.zeros_like(c_ref)
        n_ref[...] = jnp.zeros_like(n_ref)
        m_ref[...] = jnp.zeros_like(m_ref)

    xe_ref[tail:tail + L, 0:W] = q_ref[...]
    xe_ref[tail:tail + L, W:2 * W] = k_ref[...]
    conv = jnp.zeros((L, 2 * W), F32)
    for j in range(CONV_WIDTH):
        off = tail - (CONV_WIDTH - 1) + j
        conv = conv + convw_ref[j:j + 1, :] * xe_ref[off:off + L, :]
    xe_ref[0:tail, :] = xe_ref[L:L + tail, :]
    qk = conv * _sigmoid(conv)

    g = gates_ref[...] + bias_ref[...]
    g_t = g.T
    row = lax.broadcasted_iota(jnp.int32, (L, L), 0)
    col = lax.broadcasted_iota(jnp.int32, (L, L), 1)
    causal = col <= row
    tri = jnp.where(causal, 1.0, 0.0).astype(F32)
    tri_t = jnp.where(row <= col, 1.0, 0.0).astype(F32)
    hi = lax.Precision.HIGHEST
    a_cols = jnp.dot(tri, _log_sigmoid(g), precision=hi, preferred_element_type=F32)
    a_rows = jnp.dot(_log_sigmoid(g_t), tri_t, precision=hi, preferred_element_type=F32)

    scale = np.float32(Dh ** -0.5)
    for h in range(H):
        a_c = a_cols[:, H + h:H + h + 1]
        a_r = a_rows[H + h:H + h + 1, :]
        li_c = g[:, h:h + 1]
        li_r = g_t[h:h + 1, :]
        a_tot = a_r[:, L - 1:L]
        m0 = m_ref[h]
        c0 = c_ref[h]
        n0 = n_ref[h]

        q = qk[:, h * Dh:(h + 1) * Dh] * scale
        k = qk[:, W + h * Dh:W + (h + 1) * Dh]
        v = v_ref[:, h * Dh:(h + 1) * Dh]
        qb, kb, vb = q.astype(BF16), k.astype(BF16), v.astype(BF16)

        dmat = jnp.where(causal, a_c - a_r + li_r, -jnp.inf)
        m_inter = a_c + m0
        m_t = jnp.maximum(m_inter, jnp.max(dmat, axis=-1, keepdims=True))
        wm = jnp.exp(dmat - m_t)
        s_inter = jnp.exp(m_inter - m_t)
        s = lax.dot_general(qb, kb, NT_DIMS, preferred_element_type=F32) * wm
        num = (jnp.dot(s.astype(BF16), vb, preferred_element_type=F32)
               + s_inter * jnp.dot(qb, c0.astype(BF16), preferred_element_type=F32))
        den = (jnp.sum(s, axis=-1, keepdims=True)
               + s_inter * jnp.sum(q * n0, axis=-1, keepdims=True))
        hh = num / jnp.maximum(jnp.abs(den), jnp.exp(-m_t))
        hh = hh * _rms_scale(hh)
        og = _sigmoid(o_ref[:, h * Dh:(h + 1) * Dh])
        hm_ref[:, h * Dh:(h + 1) * Dh] = (hh * ng_ref[:, h * Dh:(h + 1) * Dh] * og).astype(BF16)

        g_r = a_tot - a_r + li_r
        g_c = a_tot - a_c + li_c
        m_loc = jnp.max(g_r, axis=-1, keepdims=True)
        kw = k * jnp.exp(g_c - m_loc)
        d_c = jnp.dot(kw.T.astype(BF16), vb, preferred_element_type=F32)
        d_n = jnp.sum(kw, axis=0, keepdims=True)
        m_new = jnp.maximum(a_tot + m0, m_loc)
        s_old = jnp.exp(a_tot + m0 - m_new)
        s_new = jnp.exp(m_loc - m_new)
        c_ref[h] = s_old * c0 + s_new * d_c
        n_ref[h] = s_old * n0 + s_new * d_n
        m_ref[h] = m_new


def _mlstm(z, gates, bias_row, conv_w, norm_g_row, batch, seq):
    T = z.shape[0]
    L, W = CHUNK, MLSTM_WIDTH
    nc = seq // L
    blk = lambda col: pl.BlockSpec((L, W), lambda b, c, col=col: (b * nc + c, col))
    return pl.pallas_call(
        _mlstm_kernel,
        grid=(batch, nc),
        in_specs=[
            blk(0), blk(1), blk(2), blk(3),
            pl.BlockSpec((L, LANES), lambda b, c: (b * nc + c, 0)),
            pl.BlockSpec((1, LANES), lambda b, c: (0, 0)),
            pl.BlockSpec((CONV_WIDTH, 2 * W), lambda b, c: (0, 0)),
            pl.BlockSpec((1, W), lambda b, c: (0, 0)),
        ],
        out_specs=pl.BlockSpec((L, W), lambda b, c: (b * nc + c, 0)),
        out_shape=jax.ShapeDtypeStruct((T, W), BF16),
        scratch_shapes=[
            pltpu.VMEM((L + 2 * SUBLANES, 2 * W), F32),
            pltpu.VMEM((MLSTM_HEADS, HEAD_DIM, HEAD_DIM), F32),
            pltpu.VMEM((MLSTM_HEADS, 1, HEAD_DIM), F32),
            pltpu.VMEM((MLSTM_HEADS, 1, 1), F32),
        ],
        compiler_params=_params("parallel", "arbitrary"),
        name="mlstm",
    )(z, z, z, z, gates, bias_row, conv_w, norm_g_row)


S5_BLOCK = 256
S5_SEG = S5_BLOCK // SUBLANES
S5_TILES = S5_WIDTH // LANES
S5_TILE_STATE = 8 * S5_STATE


def _s5_kernel(u_ref, win_ref, wout_ref, wglu_ref, a_ref, aseg_ref, d_ref, ys_ref,
               up_ref, x_ref, xs_ref, yp_ref, carry_ref):
    nseg, seg, half = SUBLANES, S5_SEG, S5_TILE_STATE

    @pl.when(pl.program_id(2) == 0)
    def _():
        carry_ref[...] = jnp.zeros_like(carry_ref)

    for i in range(seg):
        up_ref[nseg * i:nseg * (i + 1), :] = u_ref[pl.ds(i, nseg, stride=seg), :]

    def cmul_add(ar, ai, sr, si, xr, xi):
        return ar * sr - ai * si + xr, ar * si + ai * sr + xi

    u_t = up_ref[...]
    x_ref[...] = jnp.dot(u_t.astype(BF16), win_ref[...], preferred_element_type=F32)
    ar = a_ref[:, 0:half]
    ai = a_ref[:, half:2 * half]

    fr = jnp.zeros((nseg, half), F32)
    fi = jnp.zeros((nseg, half), F32)
    for i in range(seg):
        rows = slice(nseg * i, nseg * (i + 1))
        fr, fi = cmul_add(ar, ai, fr, fi, x_ref[rows, 0:half], x_ref[rows, half:2 * half])

    lr = aseg_ref[:, 0:half]
    lim = aseg_ref[:, half:2 * half]
    cr = carry_ref[:, 0:half]
    ci = carry_ref[:, half:2 * half]
    crs, cis = [], []
    for r in range(nseg):
        crs.append(cr)
        cis.append(ci)
        cr, ci = cmul_add(lr, lim, cr, ci, fr[r:r + 1, :], fi[r:r + 1, :])
    carry_ref[:, 0:half] = cr
    carry_ref[:, half:2 * half] = ci
    sr = jnp.concatenate(crs, axis=0)
    si = jnp.concatenate(cis, axis=0)

    for i in range(seg):
        rows = slice(nseg * i, nseg * (i + 1))
        sr, si = cmul_add(ar, ai, sr, si, x_ref[rows, 0:half], x_ref[rows, half:2 * half])
        xs_ref[rows, 0:half] = sr
        xs_ref[rows, half:2 * half] = si

    y = jnp.dot(xs_ref[...].astype(BF16), wout_ref[...], preferred_element_type=F32)
    y = y + d_ref[...] * u_t
    ab = jnp.dot(_gelu(y).astype(BF16), wglu_ref[...], preferred_element_type=F32)
    yp_ref[...] = ab[:, 0:LANES] * _sigmoid(ab[:, LANES:2 * LANES])

    for i in range(seg):
        ys_ref[pl.ds(i, nseg, stride=seg), :] = yp_ref[nseg * i:nseg * (i + 1), :]


def _s5(z, win, wout, wglu, a_rows, aseg_rows, d_row, batch, seq):
    T = z.shape[0]
    blk = S5_BLOCK
    nb = seq // blk
    u_col = (4 * MLSTM_WIDTH) // LANES
    tile = lambda shape: pl.BlockSpec((None,) + shape, lambda b, j, c: (j,) + (0,) * len(shape))
    return pl.pallas_call(
        _s5_kernel,
        grid=(batch, S5_TILES, nb),
        in_specs=[
            pl.BlockSpec((blk, LANES), lambda b, j, c: (b * nb + c, u_col + j)),
            tile(win.shape[1:]), tile(wout.shape[1:]), tile(wglu.shape[1:]),
            tile(a_rows.shape[1:]), tile(aseg_rows.shape[1:]),
            pl.BlockSpec((1, LANES), lambda b, j, c: (0, j)),
        ],
        out_specs=pl.BlockSpec((blk, LANES), lambda b, j, c: (b * nb + c, j)),
        out_shape=jax.ShapeDtypeStruct((T, S5_WIDTH), F32),
        scratch_shapes=[
            pltpu.VMEM((blk, LANES), F32),
            pltpu.VMEM((blk, 2 * S5_TILE_STATE), F32),
            pltpu.VMEM((blk, 2 * S5_TILE_STATE), F32),
            pltpu.VMEM((blk, LANES), F32),
            pltpu.VMEM((1, 2 * S5_TILE_STATE), F32),
        ],
        compiler_params=_params("parallel", "parallel", "arbitrary"),
        name="s5",
    )(z, win, wout, wglu, a_rows, aseg_rows, d_row)


def _s5_weights(lam_re, lam_im, log_dt, b_re, b_im, c_re, c_im, d, glu_w):
    G, P, Hg = S5_GROUPS, S5_STATE, S5_GROUP
    nt, gl = S5_TILES, G // S5_TILES
    dt = jnp.exp(log_dt.astype(F32))[:, None]
    mag = jnp.exp(lam_re * dt)
    ar = mag * jnp.cos(lam_im * dt)
    ai = mag * jnp.sin(lam_im * dt)
    den = lam_re * lam_re + lam_im * lam_im
    fr = ((ar - 1.0) * lam_re + ai * lam_im) / den
    fi = (ai * lam_re - (ar - 1.0) * lam_im) / den
    bbr = fr[..., None] * b_re - fi[..., None] * b_im
    bbi = fr[..., None] * b_im + fi[..., None] * b_re
    eye = jnp.eye(gl, dtype=F32)

    def tiles(v):
        return v.reshape((nt, gl) + v.shape[1:])

    bb = jnp.stack([tiles(bbr), tiles(bbi)], axis=2)
    win = jnp.einsum('tgrph,gk->tghrkp', bb, eye).reshape(nt, gl * Hg, 2 * gl * P)
    cc = jnp.stack([tiles(c_re), -tiles(c_im)], axis=2)
    wout = jnp.einsum('tgrhp,gk->trkpgh', cc, eye).reshape(nt, 2 * gl * P, gl * Hg)
    gw = tiles(glu_w).reshape(nt, gl, Hg, 2, Hg)
    wglu = jnp.einsum('tghrk,gq->tqhrgk', gw, eye).reshape(nt, gl * Hg, 2 * gl * Hg)

    def rows(re, im):
        return jnp.concatenate([tiles(re).reshape(nt, 1, gl * P), tiles(im).reshape(nt, 1, gl * P)], axis=-1)

    a_rows = jnp.broadcast_to(rows(ar, ai), (nt, SUBLANES, 2 * gl * P))
    pr, pi = ar, ai
    for _ in range(int(math.log2(S5_SEG))):
        pr, pi = pr * pr - pi * pi, 2.0 * pr * pi
    aseg_rows = rows(pr, pi)
    d_row = d.reshape(1, G * Hg).astype(F32)
    return win.astype(BF16), wout.astype(BF16), wglu.astype(BF16), a_rows, aseg_rows, d_row


def _outproj_kernel(hm_ref, ys_ref, w1_ref, w2_ref, x_ref, g_ref, h1_ref, xn_ref):
    mix = (jnp.dot(hm_ref[...], w1_ref[...], preferred_element_type=F32)
           + jnp.dot(ys_ref[...].astype(BF16), w2_ref[...], preferred_element_type=F32))
    h1 = x_ref[...] + mix
    h1_ref[...] = h1
    xn_ref[...] = (h1 * _rms_scale(h1) * g_ref[...]).astype(BF16)


def _outproj(hm, ys, w1, w2, x2d, g_row, tm=512):
    T, D = x2d.shape
    W = hm.shape[1]
    row = lambda w: pl.BlockSpec((tm, w), lambda i: (i, 0))
    full = lambda shape: pl.BlockSpec(shape, lambda i: (0,) * len(shape))
    return pl.pallas_call(
        _outproj_kernel,
        grid=(T // tm,),
        in_specs=[row(W), row(W), full(w1.shape), full(w2.shape), row(D), full((1, D))],
        out_specs=[row(D), row(D)],
        out_shape=[jax.ShapeDtypeStruct((T, D), F32), jax.ShapeDtypeStruct((T, D), BF16)],
        compiler_params=_params("parallel"),
        name="outproj",
    )(hm, ys, w1, w2, x2d, g_row)


ROUTE_TOKENS = 256


def _topk_rows(s, k):
    n = s.shape[0]
    iota = lax.broadcasted_iota(jnp.int32, s.shape, 0).astype(F32)
    vals, idxs = [], []
    for _ in range(k):
        m = jnp.max(s, axis=0, keepdims=True)
        idx = jnp.min(jnp.where(s == m, iota, float(n)), axis=0, keepdims=True)
        s = jnp.where(iota == idx, -jnp.inf, s)
        vals.append(m)
        idxs.append(idx)
    return jnp.concatenate(vals, axis=0), jnp.concatenate(idxs, axis=0)


def _route_kernel(x_ref, wq_ref, sk_ref, i1_ref, i2_ref, gate_ref, q_ref):
    K, half = PEER_TOPK, PEER_QDIM // 2
    h = pl.program_id(1)

    @pl.when(h == 0)
    def _():
        q = jnp.dot(x_ref[...], wq_ref[...], preferred_element_type=F32)
        for hh in range(PEER_HEADS):
            q_ref[hh] = q[:, hh * PEER_QDIM:(hh + 1) * PEER_QDIM].astype(BF16)

    qh = q_ref[h]
    tops = []
    for p in range(2):
        st = lax.dot_general(sk_ref[p], qh[:, p * half:(p + 1) * half], NT_DIMS,
                             preferred_element_type=F32)
        tops.append(_topk_rows(st, K))
    (v1, j1), (v2, j2) = tops
    cand = jnp.concatenate([v1[r:r + 1, :] + v2 for r in range(K)], axis=0)
    cid = jnp.concatenate([j1[r:r + 1, :] * np.float32(PEER_NKEYS) + j2 for r in range(K)], axis=0)
    iota = lax.broadcasted_iota(jnp.int32, cand.shape, 0).astype(F32)
    top_s, eid = [], []
    for _ in range(K):
        m = jnp.max(cand, axis=0, keepdims=True)
        pos = jnp.min(jnp.where(cand == m, iota, float(K * K)), axis=0, keepdims=True)
        sel = iota == pos
        eid.append(jnp.sum(jnp.where(sel, cid, 0.0), axis=0, keepdims=True))
        cand = jnp.where(sel, -jnp.inf, cand)
        top_s.append(m)
    top_s = jnp.concatenate(top_s, axis=0)
    eid = jnp.concatenate(eid, axis=0)
    e = jnp.exp(top_s - top_s[0:1, :])
    gate_ref[...] = e / jnp.sum(e, axis=0, keepdims=True)
    i1 = jnp.floor(eid * np.float32(1.0 / PEER_NKEYS))
    i1_ref[...] = i1
    i2_ref[...] = eid - i1 * np.float32(PEER_NKEYS)


def _route(xn2, wq, subkeys):
    T, D = xn2.shape
    tt, K = ROUTE_TOKENS, PEER_TOPK
    out = pl.BlockSpec((K, tt), lambda i, h: (h, i))
    shape = jax.ShapeDtypeStruct((PEER_HEADS * K, T), F32)
    return pl.pallas_call(
        _route_kernel,
        grid=(T // tt, PEER_HEADS),
        in_specs=[
            pl.BlockSpec((tt, D), lambda i, h: (i, 0)),
            pl.BlockSpec(wq.shape, lambda i, h: (0, 0)),
            pl.BlockSpec(subkeys.shape, lambda i, h: (0, 0, 0)),
        ],
        out_specs=[out, out, out],
        out_shape=[shape, shape, shape],
        scratch_shapes=[pltpu.VMEM((PEER_HEADS, tt, PEER_QDIM), BF16)],
        compiler_params=_params("parallel", "arbitrary"),
        name="peer_route",
    )(xn2, wq, subkeys)


GATE_TOKENS = 128


def _gates_kernel(i1_ref, i2_ref, gate_ref, g_ref, a_ref, b_ref, w_ref):
    n = PEER_NKEYS
    a_ref[...] = i1_ref[...].T
    b_ref[...] = i2_ref[...].T
    w_ref[...] = gate_ref[...].T
    keys = lax.broadcasted_iota(jnp.int32, (n, n), 0).astype(F32)

    def body(t, carry):
        r1 = a_ref[pl.ds(t, 1), :]
        r2 = b_ref[pl.ds(t, 1), :]
        w = w_ref[pl.ds(t, 1), :]
        at = jnp.where(keys == r1, 1.0, 0.0).astype(BF16)
        bt = jnp.where(keys == r2, w, 0.0).astype(BF16)
        g_ref[t] = lax.dot_general(at, bt, NT_DIMS, preferred_element_type=F32).astype(BF16)
        return carry

    lax.fori_loop(0, GATE_TOKENS, body, 0)


def _gates(i1, i2, gate):
    slots, T = i1.shape
    tt, n = GATE_TOKENS, PEER_NKEYS
    spec = pl.BlockSpec((slots, tt), lambda i: (0, i))
    return pl.pallas_call(
        _gates_kernel,
        grid=(T // tt,),
        in_specs=[spec, spec, spec],
        out_specs=pl.BlockSpec((tt, n, n), lambda i: (i, 0, 0)),
        out_shape=jax.ShapeDtypeStruct((T, n, n), BF16),
        scratch_shapes=[pltpu.VMEM((tt, slots), F32)] * 3,
        compiler_params=_params("parallel"),
        name="peer_gates",
    )(i1, i2, gate)


def _experts_kernel(x_ref, u_ref, v_ref, g_ref, h1_ref, fg_ref, y_ref, acc_ref):
    e = pl.program_id(1)

    @pl.when(e == 0)
    def _():
        acc_ref[...] = jnp.zeros_like(acc_ref)

    act = _gelu(lax.dot_general(x_ref[...], u_ref[...], NT_DIMS, preferred_element_type=F32))
    w = (g_ref[...].astype(F32) * act).astype(BF16)
    acc_ref[...] += jnp.dot(w, v_ref[...], preferred_element_type=F32)

    @pl.when(e == pl.num_programs(1) - 1)
    def _():
        h2 = h1_ref[...] + acc_ref[...]
        y_ref[...] = h2 * _rms_scale(h2) * fg_ref[...]


def _experts(xn2, u_tab, v_tab, g2d, h1, fg_row, tm=512, te=1024):
    T, D = xn2.shape
    E = u_tab.shape[0]
    return pl.pallas_call(
        _experts_kernel,
        grid=(T // tm, E // te),
        in_specs=[
            pl.BlockSpec((tm, D), lambda i, e: (i, 0)),
            pl.BlockSpec((te, D), lambda i, e: (e, 0)),
            pl.BlockSpec((te, D), lambda i, e: (e, 0)),
            pl.BlockSpec((tm, te), lambda i, e: (i, e)),
            pl.BlockSpec((tm, D), lambda i, e: (i, 0)),
            pl.BlockSpec((1, D), lambda i, e: (0, 0)),
        ],
        out_specs=pl.BlockSpec((tm, D), lambda i, e: (i, 0)),
        out_shape=jax.ShapeDtypeStruct((T, D), F32),
        scratch_shapes=[pltpu.VMEM((tm, D), F32)],
        compiler_params=_params("parallel", "arbitrary"),
        name="peer_experts",
    )(xn2, u_tab, v_tab, g2d, h1, fg_row)


def kernel(x, norm1_g, w_in, b_gates, conv_qk_w, mlstm_norm_g, s5_lambda_re, s5_lambda_im, s5_log_dt,
           s5_b_re, s5_b_im, s5_c_re, s5_c_im, s5_d, s5_glu_w, w_out, norm2_g, peer_wq, peer_subkeys,
           peer_u, peer_v, final_g):
    B, S, D = x.shape
    W, H = MLSTM_WIDTH, MLSTM_HEADS
    depth = norm1_g.shape[0]
    h = x.reshape(B * S, D)
    for l in range(depth):
        wl = w_in[l]
        n_gate = 2 * H
        w_main = jnp.concatenate([wl[:, :4 * W], wl[:, 4 * W + n_gate:]], axis=1).astype(BF16)
        w_gates = jnp.pad(wl[:, 4 * W:4 * W + n_gate], ((0, 0), (0, LANES - n_gate))).astype(BF16)
        bias_row = jnp.pad(b_gates[l].astype(F32), (0, LANES - n_gate)).reshape(1, LANES)
        s5w = _s5_weights(s5_lambda_re[l].astype(F32), s5_lambda_im[l].astype(F32), s5_log_dt[l],
                          s5_b_re[l].astype(F32), s5_b_im[l].astype(F32), s5_c_re[l].astype(F32),
                          s5_c_im[l].astype(F32), s5_d[l], s5_glu_w[l].astype(F32))
        wo = w_out[l].astype(BF16)

        z, gates = _inproj(h, norm1_g[l].reshape(1, D).astype(F32), w_main, w_gates)
        hm = _mlstm(z, gates, bias_row, conv_qk_w[l].astype(F32),
                    mlstm_norm_g[l].reshape(1, W).astype(F32), B, S)
        ys = _s5(z, *s5w, B, S)
        h1, xn2 = _outproj(hm, ys, wo[:W], wo[W:], h, norm2_g[l].reshape(1, D).astype(F32))

        i1, i2, gate = _route(xn2, peer_wq[l].astype(BF16), peer_subkeys[l].astype(BF16))
        g3 = _gates(i1, i2, gate)
        g2d = g3.reshape(B * S, PEER_EXPERTS)
        assert l == depth - 1, "kernel is specialised to a single layer followed by the final norm"
        h = _experts(xn2, peer_u[l].astype(BF16), peer_v[l].astype(BF16), g2d, h1,
                     final_g.reshape(1, D).astype(F32))
    return h.reshape(B, S, D)
```

```python
import functools
import math

import jax
import jax.numpy as jnp
import numpy as np
from jax import lax
from jax.experimental import pallas as pl
from jax.experimental.pallas import tpu as pltpu

F32 = jnp.float32
BF16 = jnp.bfloat16

D_MODEL = 2048
MLSTM_WIDTH = 1024
MLSTM_HEADS = 4
HEAD_DIM = 256
CHUNK = 128
CONV_WIDTH = 4
S5_WIDTH = 1024
S5_GROUP = 16
S5_GROUPS = 64
S5_STATE = 64
PEER_HEADS = 8
PEER_NKEYS = 128
PEER_EXPERTS = PEER_NKEYS * PEER_NKEYS
PEER_TOPK = 16
PEER_QDIM = 256
RMS_EPS = 1e-6

LANES = 128
SUBLANES = 8
VMEM_LIMIT = 56 * 1024 * 1024

NT_DIMS = (((1,), (1,)), ((), ()))


def _params(*sem):
    return pltpu.CompilerParams(dimension_semantics=sem, vmem_limit_bytes=VMEM_LIMIT)


def _rms_scale(v):
    return lax.rsqrt(jnp.mean(v * v, axis=-1, keepdims=True) + RMS_EPS)


def _gelu(v):
    return 0.5 * v * (1.0 + lax.erf(v * np.float32(math.sqrt(0.5))))


def _sigmoid(v):
    return 1.0 / (1.0 + jnp.exp(-v))


def _log_sigmoid(v):
    return jnp.minimum(v, 0.0) - jnp.log1p(jnp.exp(-jnp.abs(v)))


def _inproj_kernel(x_ref, g_ref, w_ref, wg_ref, z_ref, gates_ref, xn_ref):
    @pl.when(pl.program_id(1) == 0)
    def _():
        x = x_ref[...]
        xn_ref[...] = (x * _rms_scale(x) * g_ref[...]).astype(BF16)
        gates_ref[...] = jnp.dot(xn_ref[...], wg_ref[...], preferred_element_type=F32)

    z_ref[...] = jnp.dot(xn_ref[...], w_ref[...], preferred_element_type=F32)


def _inproj(x2d, g_row, w_main, w_gates, tm=1024, tn=1024):
    T, D = x2d.shape
    N = w_main.shape[1]
    return pl.pallas_call(
        _inproj_kernel,
        grid=(T // tm, N // tn),
        in_specs=[
            pl.BlockSpec((tm, D), lambda i, j: (i, 0)),
            pl.BlockSpec((1, D), lambda i, j: (0, 0)),
            pl.BlockSpec((D, tn), lambda i, j: (0, j)),
            pl.BlockSpec((D, LANES), lambda i, j: (0, 0)),
        ],
        out_specs=[
            pl.BlockSpec((tm, tn), lambda i, j: (i, j)),
            pl.BlockSpec((tm, LANES), lambda i, j: (i, 0)),
        ],
        out_shape=[jax.ShapeDtypeStruct((T, N), F32), jax.ShapeDtypeStruct((T, LANES), F32)],
        scratch_shapes=[pltpu.VMEM((tm, D), BF16)],
        compiler_params=_params("parallel", "arbitrary"),
        name="inproj",
    )(x2d, g_row, w_main, w_gates)


def _mlstm_kernel(q_ref, k_ref, v_ref, o_ref, gates_ref, bias_ref, convw_ref, ng_ref, hm_ref,
                  xe_ref, c_ref, n_ref, m_ref):
    L, H, Dh, W = CHUNK, MLSTM_HEADS, HEAD_DIM, MLSTM_WIDTH
    tail = SUBLANES

    @pl.when(pl.program_id(1) == 0)
    def _():
        xe_ref[0:tail, :] = jnp.zeros((tail, 2 * W), F32)
        c_ref[...] = jnp.zeros_like(c_ref)
        n_ref[...] = jnp.zeros_like(n_ref)
        m_ref[...] = jnp.zeros_like(m_ref)

    xe_ref[tail:tail + L, 0:W] = q_ref[...]
    xe_ref[tail:tail + L, W:2 * W] = k_ref[...]
    conv = jnp.zeros((L, 2 * W), F32)
    for j in range(CONV_WIDTH):
        off = tail - (CONV_WIDTH - 1) + j
        conv = conv + convw_ref[j:j + 1, :] * xe_ref[off:off + L, :]
    xe_ref[0:tail, :] = xe_ref[L:L + tail, :]
    qk = conv * _sigmoid(conv)

    g = gates_ref[...] + bias_ref[...]
    g_t = g.T
    row = lax.broadcasted_iota(jnp.int32, (L, L), 0)
    col = lax.broadcasted_iota(jnp.int32, (L, L), 1)
    causal = col <= row
    tri = jnp.where(causal, 1.0, 0.0).astype(F32)
    tri_t = jnp.where(row <= col, 1.0, 0.0).astype(F32)
    hi = lax.Precision.HIGHEST
    a_cols = jnp.dot(tri, _log_sigmoid(g), precision=hi, preferred_element_type=F32)
    a_rows = jnp.dot(_log_sigmoid(g_t), tri_t, precision=hi, preferred_element_type=F32)

    scale = np.float32(Dh ** -0.5)
    for h in range(H):
        a_c = a_cols[:, H + h:H + h + 1]
        a_r = a_rows[H + h:H + h + 1, :]
        li_c = g[:, h:h + 1]
        li_r = g_t[h:h + 1, :]
        a_tot = a_r[:, L - 1:L]
        m0 = m_ref[h]
        c0 = c_ref[h]
        n0 = n_ref[h]

        q = qk[:, h * Dh:(h + 1) * Dh] * scale
        k = qk[:, W + h * Dh:W + (h + 1) * Dh]
        v = v_ref[:, h * Dh:(h + 1) * Dh]
        qb, kb, vb = q.astype(BF16), k.astype(BF16), v.astype(BF16)

        dmat = jnp.where(causal, a_c - a_r + li_r, -jnp.inf)
        m_inter = a_c + m0
        m_t = jnp.maximum(m_inter, jnp.max(dmat, axis=-1, keepdims=True))
        wm = jnp.exp(dmat - m_t)
        s_inter = jnp.exp(m_inter - m_t)
        s = lax.dot_general(qb, kb, NT_DIMS, preferred_element_type=F32) * wm
        num = (jnp.dot(s.astype(BF16), vb, preferred_element_type=F32)
               + s_inter * jnp.dot(qb, c0.astype(BF16), preferred_element_type=F32))
        den = (jnp.sum(s, axis=-1, keepdims=True)
               + s_inter * jnp.sum(q * n0, axis=-1, keepdims=True))
        hh = num / jnp.maximum(jnp.abs(den), jnp.exp(-m_t))
        hh = hh * _rms_scale(hh)
        og = _sigmoid(o_ref[:, h * Dh:(h + 1) * Dh])
        hm_ref[:, h * Dh:(h + 1) * Dh] = (hh * ng_ref[:, h * Dh:(h + 1) * Dh] * og).astype(BF16)

        g_r = a_tot - a_r + li_r
        g_c = a_tot - a_c + li_c
        m_loc = jnp.max(g_r, axis=-1, keepdims=True)
        kw = k * jnp.exp(g_c - m_loc)
        d_c = jnp.dot(kw.T.astype(BF16), vb, preferred_element_type=F32)
        d_n = jnp.sum(kw, axis=0, keepdims=True)
        m_new = jnp.maximum(a_tot + m0, m_loc)
        s_old = jnp.exp(a_tot + m0 - m_new)
        s_new = jnp.exp(m_loc - m_new)
        c_ref[h] = s_old * c0 + s_new * d_c
        n_ref[h] = s_old * n0 + s_new * d_n
        m_ref[h] = m_new


def _mlstm(z, gates, bias_row, conv_w, norm_g_row, batch, seq):
    T = z.shape[0]
    L, W = CHUNK, MLSTM_WIDTH
    nc = seq // L
    blk = lambda col: pl.BlockSpec((L, W), lambda b, c, col=col: (b * nc + c, col))
    return pl.pallas_call(
        _mlstm_kernel,
        grid=(batch, nc),
        in_specs=[
            blk(0), blk(1), blk(2), blk(3),
            pl.BlockSpec((L, LANES), lambda b, c: (b * nc + c, 0)),
            pl.BlockSpec((1, LANES), lambda b, c: (0, 0)),
            pl.BlockSpec((CONV_WIDTH, 2 * W), lambda b, c: (0, 0)),
            pl.BlockSpec((1, W), lambda b, c: (0, 0)),
        ],
        out_specs=pl.BlockSpec((L, W), lambda b, c: (b * nc + c, 0)),
        out_shape=jax.ShapeDtypeStruct((T, W), BF16),
        scratch_shapes=[
            pltpu.VMEM((L + 2 * SUBLANES, 2 * W), F32),
            pltpu.VMEM((MLSTM_HEADS, HEAD_DIM, HEAD_DIM), F32),
            pltpu.VMEM((MLSTM_HEADS, 1, HEAD_DIM), F32),
            pltpu.VMEM((MLSTM_HEADS, 1, 1), F32),
        ],
        compiler_params=_params("parallel", "arbitrary"),
        name="mlstm",
    )(z, z, z, z, gates, bias_row, conv_w, norm_g_row)


S5_BLOCK = 256
S5_SEG = S5_BLOCK // SUBLANES
S5_TILES = S5_WIDTH // LANES
S5_TILE_STATE = 8 * S5_STATE


def _s5_kernel(u_ref, win_ref, wout_ref, wglu_ref, a_ref, aseg_ref, d_ref, ys_ref,
               up_ref, x_ref, xs_ref, yp_ref, carry_ref):
    nseg, seg, half = SUBLANES, S5_SEG, S5_TILE_STATE

    @pl.when(pl.program_id(2) == 0)
    def _():
        carry_ref[...] = jnp.zeros_like(carry_ref)

    for i in range(seg):
        up_ref[nseg * i:nseg * (i + 1), :] = u_ref[pl.ds(i, nseg, stride=seg), :]

    def cmul_add(ar, ai, sr, si, xr, xi):
        return ar * sr - ai * si + xr, ar * si + ai * sr + xi

    u_t = up_ref[...]
    x_ref[...] = jnp.dot(u_t.astype(BF16), win_ref[...], preferred_element_type=F32)
    ar = a_ref[:, 0:half]
    ai = a_ref[:, half:2 * half]

    fr = jnp.zeros((nseg, half), F32)
    fi = jnp.zeros((nseg, half), F32)
    for i in range(seg):
        rows = slice(nseg * i, nseg * (i + 1))
        fr, fi = cmul_add(ar, ai, fr, fi, x_ref[rows, 0:half], x_ref[rows, half:2 * half])

    lr = aseg_ref[:, 0:half]
    lim = aseg_ref[:, half:2 * half]
    cr = carry_ref[:, 0:half]
    ci = carry_ref[:, half:2 * half]
    crs, cis = [], []
    for r in range(nseg):
        crs.append(cr)
        cis.append(ci)
        cr, ci = cmul_add(lr, lim, cr, ci, fr[r:r + 1, :], fi[r:r + 1, :])
    carry_ref[:, 0:half] = cr
    carry_ref[:, half:2 * half] = ci
    sr = jnp.concatenate(crs, axis=0)
    si = jnp.concatenate(cis, axis=0)

    for i in range(seg):
        rows = slice(nseg * i, nseg * (i + 1))
        sr, si = cmul_add(ar, ai, sr, si, x_ref[rows, 0:half], x_ref[rows, half:2 * half])
        xs_ref[rows, 0:half] = sr
        xs_ref[rows, half:2 * half] = si

    y = jnp.dot(xs_ref[...].astype(BF16), wout_ref[...], preferred_element_type=F32)
    y = y + d_ref[...] * u_t
    ab = jnp.dot(_gelu(y).astype(BF16), wglu_ref[...], preferred_element_type=F32)
    yp_ref[...] = ab[:, 0:LANES] * _sigmoid(ab[:, LANES:2 * LANES])

    for i in range(seg):
        ys_ref[pl.ds(i, nseg, stride=seg), :] = yp_ref[nseg * i:nseg * (i + 1), :]


def _s5(z, win, wout, wglu, a_rows, aseg_rows, d_row, batch, seq):
    T = z.shape[0]
    blk = S5_BLOCK
    nb = seq // blk
    u_col = (4 * MLSTM_WIDTH) // LANES
    tile = lambda shape: pl.BlockSpec((None,) + shape, lambda b, j, c: (j,) + (0,) * len(shape))
    return pl.pallas_call(
        _s5_kernel,
        grid=(batch, S5_TILES, nb),
        in_specs=[
            pl.BlockSpec((blk, LANES), lambda b, j, c: (b * nb + c, u_col + j)),
            tile(win.shape[1:]), tile(wout.shape[1:]), tile(wglu.shape[1:]),
            tile(a_rows.shape[1:]), tile(aseg_rows.shape[1:]),
            pl.BlockSpec((1, LANES), lambda b, j, c: (0, j)),
        ],
        out_specs=pl.BlockSpec((blk, LANES), lambda b, j, c: (b * nb + c, j)),
        out_shape=jax.ShapeDtypeStruct((T, S5_WIDTH), F32),
        scratch_shapes=[
            pltpu.VMEM((blk, LANES), F32),
            pltpu.VMEM((blk, 2 * S5_TILE_STATE), F32),
            pltpu.VMEM((blk, 2 * S5_TILE_STATE), F32),
            pltpu.VMEM((blk, LANES), F32),
            pltpu.VMEM((1, 2 * S5_TILE_STATE), F32),
        ],
        compiler_params=_params("parallel", "parallel", "arbitrary"),
        name="s5",
    )(z, win, wout, wglu, a_rows, aseg_rows, d_row)


def _s5_weights(lam_re, lam_im, log_dt, b_re, b_im, c_re, c_im, d, glu_w):
    G, P, Hg = S5_GROUPS, S5_STATE, S5_GROUP
    nt, gl = S5_TILES, G // S5_TILES
    dt = jnp.exp(log_dt.astype(F32))[:, None]
    mag = jnp.exp(lam_re * dt)
    ar = mag * jnp.cos(lam_im * dt)
    ai = mag * jnp.sin(lam_im * dt)
    den = lam_re * lam_re + lam_im * lam_im
    fr = ((ar - 1.0) * lam_re + ai * lam_im) / den
    fi = (ai * lam_re - (ar - 1.0) * lam_im) / den
    bbr = fr[..., None] * b_re - fi[..., None] * b_im
    bbi = fr[..., None] * b_im + fi[..., None] * b_re
    eye = jnp.eye(gl, dtype=F32)

    def tiles(v):
        return v.reshape((nt, gl) + v.shape[1:])

    bb = jnp.stack([tiles(bbr), tiles(bbi)], axis=2)
    win = jnp.einsum('tgrph,gk->tghrkp', bb, eye).reshape(nt, gl * Hg, 2 * gl * P)
    cc = jnp.stack([tiles(c_re), -tiles(c_im)], axis=2)
    wout = jnp.einsum('tgrhp,gk->trkpgh', cc, eye).reshape(nt, 2 * gl * P, gl * Hg)
    gw = tiles(glu_w).reshape(nt, gl, Hg, 2, Hg)
    wglu = jnp.einsum('tghrk,gq->tqhrgk', gw, eye).reshape(nt, gl * Hg, 2 * gl * Hg)

    def rows(re, im):
        return jnp.concatenate([tiles(re).reshape(nt, 1, gl * P), tiles(im).reshape(nt, 1, gl * P)], axis=-1)

    a_rows = jnp.broadcast_to(rows(ar, ai), (nt, SUBLANES, 2 * gl * P))
    pr, pi = ar, ai
    for _ in range(int(math.log2(S5_SEG))):
        pr, pi = pr * pr - pi * pi, 2.0 * pr * pi
    aseg_rows = rows(pr, pi)
    d_row = d.reshape(1, G * Hg).astype(F32)
    return win.astype(BF16), wout.astype(BF16), wglu.astype(BF16), a_rows, aseg_rows, d_row


def _outproj_kernel(hm_ref, ys_ref, w1_ref, w2_ref, x_ref, g_ref, h1_ref, xn_ref):
    mix = (jnp.dot(hm_ref[...], w1_ref[...], preferred_element_type=F32)
           + jnp.dot(ys_ref[...].astype(BF16), w2_ref[...], preferred_element_type=F32))
    h1 = x_ref[...] + mix
    h1_ref[...] = h1
    xn_ref[...] = (h1 * _rms_scale(h1) * g_ref[...]).astype(BF16)


def _outproj(hm, ys, w1, w2, x2d, g_row, tm=512):
    T, D = x2d.shape
    W = hm.shape[1]
    row = lambda w: pl.BlockSpec((tm, w), lambda i: (i, 0))
    full = lambda shape: pl.BlockSpec(shape, lambda i: (0,) * len(shape))
    return pl.pallas_call(
        _outproj_kernel,
        grid=(T // tm,),
        in_specs=[row(W), row(W), full(w1.shape), full(w2.shape), row(D), full((1, D))],
        out_specs=[row(D), row(D)],
        out_shape=[jax.ShapeDtypeStruct((T, D), F32), jax.ShapeDtypeStruct((T, D), BF16)],
        compiler_params=_params("parallel"),
        name="outproj",
    )(hm, ys, w1, w2, x2d, g_row)


ROUTE_TOKENS = 256


def _topk_rows(s, k):
    n = s.shape[0]
    iota = lax.broadcasted_iota(jnp.int32, s.shape, 0).astype(F32)
    vals, idxs = [], []
    for _ in range(k):
        m = jnp.max(s, axis=0, keepdims=True)
        idx = jnp.min(jnp.where(s == m, iota, float(n)), axis=0, keepdims=True)
        s = jnp.where(iota == idx, -jnp.inf, s)
        vals.append(m)
        idxs.append(idx)
    return jnp.concatenate(vals, axis=0), jnp.concatenate(idxs, axis=0)


def _route_kernel(x_ref, wq_ref, sk_ref, i1_ref, i2_ref, gate_ref, q_ref):
    K, half = PEER_TOPK, PEER_QDIM // 2
    h = pl.program_id(1)

    @pl.when(h == 0)
    def _():
        q = jnp.dot(x_ref[...], wq_ref[...], preferred_element_type=F32)
        for hh in range(PEER_HEADS):
            q_ref[hh] = q[:, hh * PEER_QDIM:(hh + 1) * PEER_QDIM].astype(BF16)

    qh = q_ref[h]
    tops = []
    for p in range(2):
        st = lax.dot_general(sk_ref[p], qh[:, p * half:(p + 1) * half], NT_DIMS,
                             preferred_element_type=F32)
        tops.append(_topk_rows(st, K))
    (v1, j1), (v2, j2) = tops
    cand = jnp.concatenate([v1[r:r + 1, :] + v2 for r in range(K)], axis=0)
    cid = jnp.concatenate([j1[r:r + 1, :] * np.float32(PEER_NKEYS) + j2 for r in range(K)], axis=0)
    iota = lax.broadcasted_iota(jnp.int32, cand.shape, 0).astype(F32)
    top_s, eid = [], []
    for _ in range(K):
        m = jnp.max(cand, axis=0, keepdims=True)
        pos = jnp.min(jnp.where(cand == m, iota, float(K * K)), axis=0, keepdims=True)
        sel = iota == pos
        eid.append(jnp.sum(jnp.where(sel, cid, 0.0), axis=0, keepdims=True))
        cand = jnp.where(sel, -jnp.inf, cand)
        top_s.append(m)
    top_s = jnp.concatenate(top_s, axis=0)
    eid = jnp.concatenate(eid, axis=0)
    e = jnp.exp(top_s - top_s[0:1, :])
    gate_ref[...] = e / jnp.sum(e, axis=0, keepdims=True)
    i1 = jnp.floor(eid * np.float32(1.0 / PEER_NKEYS))
    i1_ref[...] = i1
    i2_ref[...] = eid - i1 * np.float32(PEER_NKEYS)


def _route(xn2, wq, subkeys):
    T, D = xn2.shape
    tt, K = ROUTE_TOKENS, PEER_TOPK
    out = pl.BlockSpec((K, tt), lambda i, h: (h, i))
    shape = jax.ShapeDtypeStruct((PEER_HEADS * K, T), F32)
    return pl.pallas_call(
        _route_kernel,
        grid=(T // tt, PEER_HEADS),
        in_specs=[
            pl.BlockSpec((tt, D), lambda i, h: (i, 0)),
            pl.BlockSpec(wq.shape, lambda i, h: (0, 0)),
            pl.BlockSpec(subkeys.shape, lambda i, h: (0, 0, 0)),
        ],
        out_specs=[out, out, out],
        out_shape=[shape, shape, shape],
        scratch_shapes=[pltpu.VMEM((PEER_HEADS, tt, PEER_QDIM), BF16)],
        compiler_params=_params("parallel", "arbitrary"),
        name="peer_route",
    )(xn2, wq, subkeys)


GATE_TOKENS = 128
GATE_UNROLL = 8


def _gates_kernel(i1_ref, i2_ref, gate_ref, g_ref, a_ref, b_ref, w_ref):
    n = PEER_NKEYS
    a_ref[...] = i1_ref[...].T
    b_ref[...] = i2_ref[...].T
    w_ref[...] = gate_ref[...].T
    keys = lax.broadcasted_iota(jnp.int32, (n, n), 0).astype(F32)

    def body(step, carry):
        base = pl.multiple_of(step * GATE_UNROLL, GATE_UNROLL)
        for u in range(GATE_UNROLL):
            t = base + u
            r1 = a_ref[pl.ds(t, 1), :]
            r2 = b_ref[pl.ds(t, 1), :]
            w = w_ref[pl.ds(t, 1), :]
            at = jnp.where(keys == r1, 1.0, 0.0).astype(BF16)
            bt = jnp.where(keys == r2, w, 0.0).astype(BF16)
            g_ref[t] = lax.dot_general(at, bt, NT_DIMS, preferred_element_type=F32)
        return carry

    lax.fori_loop(0, GATE_TOKENS // GATE_UNROLL, body, 0)


def _gates(i1, i2, gate):
    slots, T = i1.shape
    tt, n = GATE_TOKENS, PEER_NKEYS
    spec = pl.BlockSpec((slots, tt), lambda i: (0, i))
    return pl.pallas_call(
        _gates_kernel,
        grid=(T // tt,),
        in_specs=[spec, spec, spec],
        out_specs=pl.BlockSpec((tt, n, n), lambda i: (i, 0, 0)),
        out_shape=jax.ShapeDtypeStruct((T, n, n), F32),
        scratch_shapes=[pltpu.VMEM((tt, slots), F32)] * 3,
        compiler_params=_params("parallel"),
        name="peer_gates",
    )(i1, i2, gate)


def _experts_kernel(x_ref, u_ref, v_ref, g_ref, h1_ref, fg_ref, y_ref, acc_ref):
    e = pl.program_id(1)

    @pl.when(e == 0)
    def _():
        acc_ref[...] = jnp.zeros_like(acc_ref)

    act = _gelu(lax.dot_general(x_ref[...], u_ref[...], NT_DIMS, preferred_element_type=F32))
    n = PEER_NKEYS
    tm, nc, _ = g_ref.shape
    g_rows = g_ref.reshape(tm * nc, n)
    w = jnp.concatenate([(g_rows[pl.ds(c, tm, stride=nc), :] * act[:, c * n:(c + 1) * n]).astype(BF16)
                         for c in range(nc)], axis=1)
    acc_ref[...] += jnp.dot(w, v_ref[...], preferred_element_type=F32)

    @pl.when(e == pl.num_programs(1) - 1)
    def _():
        h2 = h1_ref[...] + acc_ref[...]
        y_ref[...] = h2 * _rms_scale(h2) * fg_ref[...]


def _experts(xn2, u_tab, v_tab, g3, h1, fg_row, tm=512, te=1024):
    T, D = xn2.shape
    E = u_tab.shape[0]
    n = PEER_NKEYS
    return pl.pallas_call(
        _experts_kernel,
        grid=(T // tm, E // te),
        in_specs=[
            pl.BlockSpec((tm, D), lambda i, e: (i, 0)),
            pl.BlockSpec((te, D), lambda i, e: (e, 0)),
            pl.BlockSpec((te, D), lambda i, e: (e, 0)),
            pl.BlockSpec((tm, te // n, n), lambda i, e: (i, e, 0)),
            pl.BlockSpec((tm, D), lambda i, e: (i, 0)),
            pl.BlockSpec((1, D), lambda i, e: (0, 0)),
        ],
        out_specs=pl.BlockSpec((tm, D), lambda i, e: (i, 0)),
        out_shape=jax.ShapeDtypeStruct((T, D), F32),
        scratch_shapes=[pltpu.VMEM((tm, D), F32)],
        compiler_params=_params("parallel", "arbitrary"),
        name="peer_experts",
    )(xn2, u_tab, v_tab, g3, h1, fg_row)


def kernel(x, norm1_g, w_in, b_gates, conv_qk_w, mlstm_norm_g, s5_lambda_re, s5_lambda_im, s5_log_dt,
           s5_b_re, s5_b_im, s5_c_re, s5_c_im, s5_d, s5_glu_w, w_out, norm2_g, peer_wq, peer_subkeys,
           peer_u, peer_v, final_g):
    B, S, D = x.shape
    W, H = MLSTM_WIDTH, MLSTM_HEADS
    depth = norm1_g.shape[0]
    h = x.reshape(B * S, D)
    for l in range(depth):
        wl = w_in[l]
        n_gate = 2 * H
        w_main = jnp.concatenate([wl[:, :4 * W], wl[:, 4 * W + n_gate:]], axis=1).astype(BF16)
        w_gates = jnp.pad(wl[:, 4 * W:4 * W + n_gate], ((0, 0), (0, LANES - n_gate))).astype(BF16)
        bias_row = jnp.pad(b_gates[l].astype(F32), (0, LANES - n_gate)).reshape(1, LANES)
        s5w = _s5_weights(s5_lambda_re[l].astype(F32), s5_lambda_im[l].astype(F32), s5_log_dt[l],
                          s5_b_re[l].astype(F32), s5_b_im[l].astype(F32), s5_c_re[l].astype(F32),
                          s5_c_im[l].astype(F32), s5_d[l], s5_glu_w[l].astype(F32))
        wo = w_out[l].astype(BF16)

        z, gates = _inproj(h, norm1_g[l].reshape(1, D).astype(F32), w_main, w_gates)
        hm = _mlstm(z, gates, bias_row, conv_qk_w[l].astype(F32),
                    mlstm_norm_g[l].reshape(1, W).astype(F32), B, S)
        ys = _s5(z, *s5w, B, S)
        h1, xn2 = _outproj(hm, ys, wo[:W], wo[W:], h, norm2_g[l].reshape(1, D).astype(F32))

        i1, i2, gate = _route(xn2, peer_wq[l].astype(BF16), peer_subkeys[l].astype(BF16))
        g3 = _gates(i1, i2, gate)
        assert l == depth - 1, "kernel is specialised to a single layer followed by the final norm"
        h = _experts(xn2, peer_u[l].astype(BF16), peer_v[l].astype(BF16), g3, h1,
                     final_g.reshape(1, D).astype(F32))
    return h.reshape(B, S, D)
```

```python
import functools
import math

import jax
import jax.numpy as jnp
import numpy as np
from jax import lax
from jax.experimental import pallas as pl
from jax.experimental.pallas import tpu as pltpu

F32 = jnp.float32
BF16 = jnp.bfloat16

D_MODEL = 2048
MLSTM_WIDTH = 1024
MLSTM_HEADS = 4
HEAD_DIM = 256
CHUNK = 128
CONV_WIDTH = 4
S5_WIDTH = 1024
S5_GROUP = 16
S5_GROUPS = 64
S5_STATE = 64
PEER_HEADS = 8
PEER_NKEYS = 128
PEER_EXPERTS = PEER_NKEYS * PEER_NKEYS
PEER_TOPK = 16
PEER_QDIM = 256
RMS_EPS = 1e-6

LANES = 128
SUBLANES = 8
VMEM_LIMIT = 56 * 1024 * 1024

NT_DIMS = (((1,), (1,)), ((), ()))


def _params(*sem):
    return pltpu.CompilerParams(dimension_semantics=sem, vmem_limit_bytes=VMEM_LIMIT)


def _rms_scale(v):
    return lax.rsqrt(jnp.mean(v * v, axis=-1, keepdims=True) + RMS_EPS)


def _gelu(v):
    return 0.5 * v * (1.0 + lax.erf(v * np.float32(math.sqrt(0.5))))


def _sigmoid(v):
    return 1.0 / (1.0 + jnp.exp(-v))


def _log_sigmoid(v):
    return jnp.minimum(v, 0.0) - jnp.log1p(jnp.exp(-jnp.abs(v)))


def _inproj_kernel(x_ref, g_ref, w_ref, wg_ref, z_ref, gates_ref, xn_ref):
    @pl.when(pl.program_id(1) == 0)
    def _():
        x = x_ref[...]
        xn_ref[...] = (x * _rms_scale(x) * g_ref[...]).astype(BF16)
        gates_ref[...] = jnp.dot(xn_ref[...], wg_ref[...], preferred_element_type=F32)

    z_ref[...] = jnp.dot(xn_ref[...], w_ref[...], preferred_element_type=F32)


def _inproj(x2d, g_row, w_main, w_gates, tm=1024, tn=1024):
    T, D = x2d.shape
    N = w_main.shape[1]
    return pl.pallas_call(
        _inproj_kernel,
        grid=(T // tm, N // tn),
        in_specs=[
            pl.BlockSpec((tm, D), lambda i, j: (i, 0)),
            pl.BlockSpec((1, D), lambda i, j: (0, 0)),
            pl.BlockSpec((D, tn), lambda i, j: (0, j)),
            pl.BlockSpec((D, LANES), lambda i, j: (0, 0)),
        ],
        out_specs=[
            pl.BlockSpec((tm, tn), lambda i, j: (i, j)),
            pl.BlockSpec((tm, LANES), lambda i, j: (i, 0)),
        ],
        out_shape=[jax.ShapeDtypeStruct((T, N), F32), jax.ShapeDtypeStruct((T, LANES), F32)],
        scratch_shapes=[pltpu.VMEM((tm, D), BF16)],
        compiler_params=_params("parallel", "arbitrary"),
        name="inproj",
    )(x2d, g_row, w_main, w_gates)


def _mlstm_kernel(q_ref, k_ref, v_ref, o_ref, gates_ref, bias_ref, convw_ref, ng_ref, hm_ref,
                  xe_ref, c_ref, n_ref, m_ref):
    L, H, Dh, W = CHUNK, MLSTM_HEADS, HEAD_DIM, MLSTM_WIDTH
    tail = SUBLANES

    @pl.when(pl.program_id(1) == 0)
    def _():
        xe_ref[0:tail, :] = jnp.zeros((tail, 2 * W), F32)
        c_ref[...] = jnp.zeros_like(c_ref)
        n_ref[...] = jnp.zeros_like(n_ref)
        m_ref[...] = jnp.zeros_like(m_ref)

    xe_ref[tail:tail + L, 0:W] = q_ref[...]
    xe_ref[tail:tail + L, W:2 * W] = k_ref[...]
    conv = jnp.zeros((L, 2 * W), F32)
    for j in range(CONV_WIDTH):
        off = tail - (CONV_WIDTH - 1) + j
        conv = conv + convw_ref[j:j + 1, :] * xe_ref[off:off + L, :]
    xe_ref[0:tail, :] = xe_ref[L:L + tail, :]
    qk = conv * _sigmoid(conv)

    g = gates_ref[...] + bias_ref[...]
    g_t = g.T
    row = lax.broadcasted_iota(jnp.int32, (L, L), 0)
    col = lax.broadcasted_iota(jnp.int32, (L, L), 1)
    causal = col <= row
    tri = jnp.where(causal, 1.0, 0.0).astype(F32)
    tri_t = jnp.where(row <= col, 1.0, 0.0).astype(F32)
    hi = lax.Precision.HIGHEST
    a_cols = jnp.dot(tri, _log_sigmoid(g), precision=hi, preferred_element_type=F32)
    a_rows = jnp.dot(_log_sigmoid(g_t), tri_t, precision=hi, preferred_element_type=F32)

    scale = np.float32(Dh ** -0.5)
    for h in range(H):
        a_c = a_cols[:, H + h:H + h + 1]
        a_r = a_rows[H + h:H + h + 1, :]
        li_c = g[:, h:h + 1]
        li_r = g_t[h:h + 1, :]
        a_tot = a_r[:, L - 1:L]
        m0 = m_ref[h]
        c0 = c_ref[h]
        n0 = n_ref[h]

        q = qk[:, h * Dh:(h + 1) * Dh] * scale
        k = qk[:, W + h * Dh:W + (h + 1) * Dh]
        v = v_ref[:, h * Dh:(h + 1) * Dh]
        qb, kb, vb = q.astype(BF16), k.astype(BF16), v.astype(BF16)

        dmat = jnp.where(causal, a_c - a_r + li_r, -jnp.inf)
        m_inter = a_c + m0
        m_t = jnp.maximum(m_inter, jnp.max(dmat, axis=-1, keepdims=True))
        wm = jnp.exp(dmat - m_t)
        s_inter = jnp.exp(m_inter - m_t)
        s = lax.dot_general(qb, kb, NT_DIMS, preferred_element_type=F32) * wm
        num = (jnp.dot(s.astype(BF16), vb, preferred_element_type=F32)
               + s_inter * jnp.dot(qb, c0.astype(BF16), preferred_element_type=F32))
        den = (jnp.sum(s, axis=-1, keepdims=True)
               + s_inter * jnp.sum(q * n0, axis=-1, keepdims=True))
        hh = num / jnp.maximum(jnp.abs(den), jnp.exp(-m_t))
        hh = hh * _rms_scale(hh)
        og = _sigmoid(o_ref[:, h * Dh:(h + 1) * Dh])
        hm_ref[:, h * Dh:(h + 1) * Dh] = (hh * ng_ref[:, h * Dh:(h + 1) * Dh] * og).astype(BF16)

        g_r = a_tot - a_r + li_r
        g_c = a_tot - a_c + li_c
        m_loc = jnp.max(g_r, axis=-1, keepdims=True)
        kw = k * jnp.exp(g_c - m_loc)
        d_c = jnp.dot(kw.T.astype(BF16), vb, preferred_element_type=F32)
        d_n = jnp.sum(kw, axis=0, keepdims=True)
        m_new = jnp.maximum(a_tot + m0, m_loc)
        s_old = jnp.exp(a_tot + m0 - m_new)
        s_new = jnp.exp(m_loc - m_new)
        c_ref[h] = s_old * c0 + s_new * d_c
        n_ref[h] = s_old * n0 + s_new * d_n
        m_ref[h] = m_new


def _mlstm(z, gates, bias_row, conv_w, norm_g_row, batch, seq):
    T = z.shape[0]
    L, W = CHUNK, MLSTM_WIDTH
    nc = seq // L
    blk = lambda col: pl.BlockSpec((L, W), lambda b, c, col=col: (b * nc + c, col))
    return pl.pallas_call(
        _mlstm_kernel,
        grid=(batch, nc),
        in_specs=[
            blk(0), blk(1), blk(2), blk(3),
            pl.BlockSpec((L, LANES), lambda b, c: (b * nc + c, 0)),
            pl.BlockSpec((1, LANES), lambda b, c: (0, 0)),
            pl.BlockSpec((CONV_WIDTH, 2 * W), lambda b, c: (0, 0)),
            pl.BlockSpec((1, W), lambda b, c: (0, 0)),
        ],
        out_specs=pl.BlockSpec((L, W), lambda b, c: (b * nc + c, 0)),
        out_shape=jax.ShapeDtypeStruct((T, W), BF16),
        scratch_shapes=[
            pltpu.VMEM((L + 2 * SUBLANES, 2 * W), F32),
            pltpu.VMEM((MLSTM_HEADS, HEAD_DIM, HEAD_DIM), F32),
            pltpu.VMEM((MLSTM_HEADS, 1, HEAD_DIM), F32),
            pltpu.VMEM((MLSTM_HEADS, 1, 1), F32),
        ],
        compiler_params=_params("parallel", "arbitrary"),
        name="mlstm",
    )(z, z, z, z, gates, bias_row, conv_w, norm_g_row)


S5_BLOCK = 256
S5_SEG = S5_BLOCK // SUBLANES
S5_TILES = S5_WIDTH // LANES
S5_TILE_STATE = 8 * S5_STATE


def _s5_kernel(u_ref, win_ref, wout_ref, wglu_ref, a_ref, aseg_ref, d_ref, ys_ref,
               up_ref, x_ref, xs_ref, yp_ref, carry_ref):
    nseg, seg, half = SUBLANES, S5_SEG, S5_TILE_STATE

    @pl.when(pl.program_id(2) == 0)
    def _():
        carry_ref[...] = jnp.zeros_like(carry_ref)

    for i in range(seg):
        up_ref[nseg * i:nseg * (i + 1), :] = u_ref[pl.ds(i, nseg, stride=seg), :]

    def cmul_add(ar, ai, sr, si, xr, xi):
        return ar * sr - ai * si + xr, ar * si + ai * sr + xi

    u_t = up_ref[...]
    x_ref[...] = jnp.dot(u_t.astype(BF16), win_ref[...], preferred_element_type=F32)
    ar = a_ref[:, 0:half]
    ai = a_ref[:, half:2 * half]

    fr = jnp.zeros((nseg, half), F32)
    fi = jnp.zeros((nseg, half), F32)
    for i in range(seg):
        rows = slice(nseg * i, nseg * (i + 1))
        fr, fi = cmul_add(ar, ai, fr, fi, x_ref[rows, 0:half], x_ref[rows, half:2 * half])

    lr = aseg_ref[:, 0:half]
    lim = aseg_ref[:, half:2 * half]
    cr = carry_ref[:, 0:half]
    ci = carry_ref[:, half:2 * half]
    crs, cis = [], []
    for r in range(nseg):
        crs.append(cr)
        cis.append(ci)
        cr, ci = cmul_add(lr, lim, cr, ci, fr[r:r + 1, :], fi[r:r + 1, :])
    carry_ref[:, 0:half] = cr
    carry_ref[:, half:2 * half] = ci
    sr = jnp.concatenate(crs, axis=0)
    si = jnp.concatenate(cis, axis=0)

    for i in range(seg):
        rows = slice(nseg * i, nseg * (i + 1))
        sr, si = cmul_add(ar, ai, sr, si, x_ref[rows, 0:half], x_ref[rows, half:2 * half])
        xs_ref[rows, 0:half] = sr
        xs_ref[rows, half:2 * half] = si

    y = jnp.dot(xs_ref[...].astype(BF16), wout_ref[...], preferred_element_type=F32)
    y = y + d_ref[...] * u_t
    ab = jnp.dot(_gelu(y).astype(BF16), wglu_ref[...], preferred_element_type=F32)
    yp_ref[...] = ab[:, 0:LANES] * _sigmoid(ab[:, LANES:2 * LANES])

    for i in range(seg):
        ys_ref[pl.ds(i, nseg, stride=seg), :] = yp_ref[nseg * i:nseg * (i + 1), :]


def _s5(z, win, wout, wglu, a_rows, aseg_rows, d_row, batch, seq):
    T = z.shape[0]
    blk = S5_BLOCK
    nb = seq // blk
    u_col = (4 * MLSTM_WIDTH) // LANES
    tile = lambda shape: pl.BlockSpec((None,) + shape, lambda b, j, c: (j,) + (0,) * len(shape))
    return pl.pallas_call(
        _s5_kernel,
        grid=(batch, S5_TILES, nb),
        in_specs=[
            pl.BlockSpec((blk, LANES), lambda b, j, c: (b * nb + c, u_col + j)),
            tile(win.shape[1:]), tile(wout.shape[1:]), tile(wglu.shape[1:]),
            tile(a_rows.shape[1:]), tile(aseg_rows.shape[1:]),
            pl.BlockSpec((1, LANES), lambda b, j, c: (0, j)),
        ],
        out_specs=pl.BlockSpec((blk, LANES), lambda b, j, c: (b * nb + c, j)),
        out_shape=jax.ShapeDtypeStruct((T, S5_WIDTH), F32),
        scratch_shapes=[
            pltpu.VMEM((blk, LANES), F32),
            pltpu.VMEM((blk, 2 * S5_TILE_STATE), F32),
            pltpu.VMEM((blk, 2 * S5_TILE_STATE), F32),
            pltpu.VMEM((blk, LANES), F32),
            pltpu.VMEM((1, 2 * S5_TILE_STATE), F32),
        ],
        compiler_params=_params("parallel", "parallel", "arbitrary"),
        name="s5",
    )(z, win, wout, wglu, a_rows, aseg_rows, d_row)


def _s5_weights(lam_re, lam_im, log_dt, b_re, b_im, c_re, c_im, d, glu_w):
    G, P, Hg = S5_GROUPS, S5_STATE, S5_GROUP
    nt, gl = S5_TILES, G // S5_TILES
    dt = jnp.exp(log_dt.astype(F32))[:, None]
    mag = jnp.exp(lam_re * dt)
    ar = mag * jnp.cos(lam_im * dt)
    ai = mag * jnp.sin(lam_im * dt)
    den = lam_re * lam_re + lam_im * lam_im
    fr = ((ar - 1.0) * lam_re + ai * lam_im) / den
    fi = (ai * lam_re - (ar - 1.0) * lam_im) / den
    bbr = fr[..., None] * b_re - fi[..., None] * b_im
    bbi = fr[..., None] * b_im + fi[..., None] * b_re
    eye = jnp.eye(gl, dtype=F32)

    def tiles(v):
        return v.reshape((nt, gl) + v.shape[1:])

    bb = jnp.stack([tiles(bbr), tiles(bbi)], axis=2)
    win = jnp.einsum('tgrph,gk->tghrkp', bb, eye).reshape(nt, gl * Hg, 2 * gl * P)
    cc = jnp.stack([tiles(c_re), -tiles(c_im)], axis=2)
    wout = jnp.einsum('tgrhp,gk->trkpgh', cc, eye).reshape(nt, 2 * gl * P, gl * Hg)
    gw = tiles(glu_w).reshape(nt, gl, Hg, 2, Hg)
    wglu = jnp.einsum('tghrk,gq->tqhrgk', gw, eye).reshape(nt, gl * Hg, 2 * gl * Hg)

    def rows(re, im):
        return jnp.concatenate([tiles(re).reshape(nt, 1, gl * P), tiles(im).reshape(nt, 1, gl * P)], axis=-1)

    a_rows = jnp.broadcast_to(rows(ar, ai), (nt, SUBLANES, 2 * gl * P))
    pr, pi = ar, ai
    for _ in range(int(math.log2(S5_SEG))):
        pr, pi = pr * pr - pi * pi, 2.0 * pr * pi
    aseg_rows = rows(pr, pi)
    d_row = d.reshape(1, G * Hg).astype(F32)
    return win.astype(BF16), wout.astype(BF16), wglu.astype(BF16), a_rows, aseg_rows, d_row


def _outproj_kernel(hm_ref, ys_ref, w1_ref, w2_ref, x_ref, g_ref, h1_ref, xn_ref):
    mix = (jnp.dot(hm_ref[...], w1_ref[...], preferred_element_type=F32)
           + jnp.dot(ys_ref[...].astype(BF16), w2_ref[...], preferred_element_type=F32))
    h1 = x_ref[...] + mix
    h1_ref[...] = h1
    xn_ref[...] = (h1 * _rms_scale(h1) * g_ref[...]).astype(BF16)


def _outproj(hm, ys, w1, w2, x2d, g_row, tm=512):
    T, D = x2d.shape
    W = hm.shape[1]
    row = lambda w: pl.BlockSpec((tm, w), lambda i: (i, 0))
    full = lambda shape: pl.BlockSpec(shape, lambda i: (0,) * len(shape))
    return pl.pallas_call(
        _outproj_kernel,
        grid=(T // tm,),
        in_specs=[row(W), row(W), full(w1.shape), full(w2.shape), row(D), full((1, D))],
        out_specs=[row(D), row(D)],
        out_shape=[jax.ShapeDtypeStruct((T, D), F32), jax.ShapeDtypeStruct((T, D), BF16)],
        compiler_params=_params("parallel"),
        name="outproj",
    )(hm, ys, w1, w2, x2d, g_row)


ROUTE_TOKENS = SUBLANES * LANES
ROUTE_CHUNKS = ROUTE_TOKENS // LANES


def _sort_pairs(n):
    pairs = []
    p = 1
    while p < n:
        k = p
        while k >= 1:
            for j in range(k % p, n - k, 2 * k):
                for i in range(min(k, n - j - k)):
                    if (i + j) // (2 * p) == (i + j + k) // (2 * p):
                        pairs.append((i + j, i + j + k))
            k //= 2
        p *= 2
    return pairs


def _first_second(a, b):
    (av, ai), (bv, bi) = a, b
    a_first = (av > bv) | ((av == bv) & (ai < bi))
    return ((jnp.maximum(av, bv), jnp.where(a_first, ai, bi)),
            (jnp.minimum(av, bv), jnp.where(a_first, bi, ai)))


def _sorted_group(items):
    items = list(items)
    for i, j in _sort_pairs(len(items)):
        items[i], items[j] = _first_second(items[i], items[j])
    return items


def _merge_top(a, b):
    n = len(a)
    c = [_first_second(a[v], b[n - 1 - v])[0] for v in range(n)]
    j = n // 2
    while j >= 1:
        for i in range(n):
            if (i & j) == 0:
                c[i], c[i + j] = _first_second(c[i], c[i + j])
        j //= 2
    return c


def _top_sorted(groups):
    groups = list(groups)
    while len(groups) > 1:
        groups = [_merge_top(groups[i], groups[i + 1]) for i in range(0, len(groups), 2)]
    return groups[0]


def _product_top(top1, top2):
    K = PEER_TOPK
    pairs = [(r1, r2) for r1 in range(K) for r2 in range(K) if (r1 + 1) * (r2 + 1) <= K]
    cands = [(top1[r1][0] + top2[r2][0], jnp.full_like(top1[0][0], float(r1 * K + r2))) for r1, r2 in pairs]
    pad = (jnp.full_like(top1[0][0], -jnp.inf), jnp.full_like(top1[0][0], float(K * K)))
    cands += [pad] * (-len(cands) % K)
    groups = [cands[:K]] + [_sorted_group(cands[g:g + K]) for g in range(K, len(cands), K)]
    best = _top_sorted(groups)
    top_s = [v for v, _ in best]
    e = [jnp.exp(v - top_s[0]) for v in top_s]
    z = functools.reduce(lambda x, y: x + y, e)
    gates, i1s, i2s = [], [], []
    for r in range(K):
        pos = best[r][1]
        r1 = jnp.floor(pos * np.float32(1.0 / K))
        r2 = pos - r1 * np.float32(K)
        i1, i2 = top1[0][1], top2[0][1]
        for a in range(1, K):
            i1 = jnp.where(r1 == float(a), top1[a][1], i1)
            i2 = jnp.where(r2 == float(a), top2[a][1], i2)
        gates.append(e[r] / z)
        i1s.append(i1)
        i2s.append(i2)
    return i1s, i2s, gates


def _route_kernel(x_ref, wq_ref, sk_ref, i1_ref, i2_ref, gate_ref, q_ref, sc_ref):
    K, n, half = PEER_TOPK, PEER_NKEYS, PEER_QDIM // 2
    h = pl.program_id(1)

    @pl.when(h == 0)
    def _():
        q = jnp.dot(x_ref[...], wq_ref[...], preferred_element_type=F32)
        for hh in range(PEER_HEADS):
            q_ref[hh] = q[:, hh * PEER_QDIM:(hh + 1) * PEER_QDIM].astype(BF16)

    tops = []
    for p in range(2):
        for c in range(ROUTE_CHUNKS):
            qc = q_ref[h, c * LANES:(c + 1) * LANES, p * half:(p + 1) * half]
            sc_ref[pl.ds(c, n, stride=ROUTE_CHUNKS), :] = lax.dot_general(
                sk_ref[p], qc, NT_DIMS, preferred_element_type=F32)
        items = [(sc_ref[k * SUBLANES:(k + 1) * SUBLANES, :], jnp.full((SUBLANES, LANES), float(k), F32))
                 for k in range(n)]
        tops.append(_top_sorted([_sorted_group(items[g:g + K]) for g in range(0, n, K)]))

    i1s, i2s, gates = _product_top(*tops)
    for ref, vals in ((i1_ref, i1s), (i2_ref, i2s), (gate_ref, gates)):
        rows = ref.reshape(ROUTE_CHUNKS * K, LANES)
        for r in range(K):
            rows[pl.ds(r, ROUTE_CHUNKS, stride=K), :] = vals[r]


def _route(xn2, wq, subkeys):
    T, D = xn2.shape
    tt, K = ROUTE_TOKENS, PEER_TOPK
    out = pl.BlockSpec((ROUTE_CHUNKS, K, LANES), lambda i, h: (i, h, 0))
    shape = jax.ShapeDtypeStruct((T // LANES, PEER_HEADS * K, LANES), F32)
    return pl.pallas_call(
        _route_kernel,
        grid=(T // tt, PEER_HEADS),
        in_specs=[
            pl.BlockSpec((tt, D), lambda i, h: (i, 0)),
            pl.BlockSpec(wq.shape, lambda i, h: (0, 0)),
            pl.BlockSpec(subkeys.shape, lambda i, h: (0, 0, 0)),
        ],
        out_specs=[out, out, out],
        out_shape=[shape, shape, shape],
        scratch_shapes=[pltpu.VMEM((PEER_HEADS, tt, PEER_QDIM), BF16),
                        pltpu.VMEM((PEER_NKEYS * ROUTE_CHUNKS, LANES), F32)],
        compiler_params=_params("parallel", "arbitrary"),
        name="peer_route",
    )(xn2, wq, subkeys)


GATE_TOKENS = LANES
GATE_UNROLL = 8


def _gates_kernel(i1_ref, i2_ref, gate_ref, g_ref, a_ref, b_ref, w_ref):
    n = PEER_NKEYS
    a_ref[...] = i1_ref[...].T
    b_ref[...] = i2_ref[...].T
    w_ref[...] = gate_ref[...].T
    keys = lax.broadcasted_iota(jnp.int32, (n, n), 0).astype(F32)

    def body(step, carry):
        base = pl.multiple_of(step * GATE_UNROLL, GATE_UNROLL)
        for u in range(GATE_UNROLL):
            t = base + u
            r1 = a_ref[pl.ds(t, 1), :]
            r2 = b_ref[pl.ds(t, 1), :]
            w = w_ref[pl.ds(t, 1), :]
            at = jnp.where(keys == r1, 1.0, 0.0).astype(BF16)
            bt = jnp.where(keys == r2, w, 0.0).astype(BF16)
            g_ref[t] = lax.dot_general(at, bt, NT_DIMS, preferred_element_type=F32)
        return carry

    lax.fori_loop(0, GATE_TOKENS // GATE_UNROLL, body, 0)


def _gates(i1, i2, gate):
    nchunk, slots, tt = i1.shape
    n = PEER_NKEYS
    spec = pl.BlockSpec((None, slots, tt), lambda i: (i, 0, 0))
    return pl.pallas_call(
        _gates_kernel,
        grid=(nchunk,),
        in_specs=[spec, spec, spec],
        out_specs=pl.BlockSpec((tt, n, n), lambda i: (i, 0, 0)),
        out_shape=jax.ShapeDtypeStruct((nchunk * tt, n, n), F32),
        scratch_shapes=[pltpu.VMEM((tt, slots), F32)] * 3,
        compiler_params=_params("parallel"),
        name="peer_gates",
    )(i1, i2, gate)


def _experts_kernel(x_ref, u_ref, v_ref, g_ref, h1_ref, fg_ref, y_ref, acc_ref):
    e = pl.program_id(1)

    @pl.when(e == 0)
    def _():
        acc_ref[...] = jnp.zeros_like(acc_ref)

    act = _gelu(lax.dot_general(x_ref[...], u_ref[...], NT_DIMS, preferred_element_type=F32))
    n = PEER_NKEYS
    tm, nc, _ = g_ref.shape
    g_rows = g_ref.reshape(tm * nc, n)
    w = jnp.concatenate([(g_rows[pl.ds(c, tm, stride=nc), :] * act[:, c * n:(c + 1) * n]).astype(BF16)
                         for c in range(nc)], axis=1)
    acc_ref[...] += jnp.dot(w, v_ref[...], preferred_element_type=F32)

    @pl.when(e == pl.num_programs(1) - 1)
    def _():
        h2 = h1_ref[...] + acc_ref[...]
        y_ref[...] = h2 * _rms_scale(h2) * fg_ref[...]


def _experts(xn2, u_tab, v_tab, g3, h1, fg_row, tm=512, te=1024):
    T, D = xn2.shape
    E = u_tab.shape[0]
    n = PEER_NKEYS
    return pl.pallas_call(
        _experts_kernel,
        grid=(T // tm, E // te),
        in_specs=[
            pl.BlockSpec((tm, D), lambda i, e: (i, 0)),
            pl.BlockSpec((te, D), lambda i, e: (e, 0)),
            pl.BlockSpec((te, D), lambda i, e: (e, 0)),
            pl.BlockSpec((tm, te // n, n), lambda i, e: (i, e, 0)),
            pl.BlockSpec((tm, D), lambda i, e: (i, 0)),
            pl.BlockSpec((1, D), lambda i, e: (0, 0)),
        ],
        out_specs=pl.BlockSpec((tm, D), lambda i, e: (i, 0)),
        out_shape=jax.ShapeDtypeStruct((T, D), F32),
        scratch_shapes=[pltpu.VMEM((tm, D), F32)],
        compiler_params=_params("parallel", "arbitrary"),
        name="peer_experts",
    )(xn2, u_tab, v_tab, g3, h1, fg_row)


def kernel(x, norm1_g, w_in, b_gates, conv_qk_w, mlstm_norm_g, s5_lambda_re, s5_lambda_im, s5_log_dt,
           s5_b_re, s5_b_im, s5_c_re, s5_c_im, s5_d, s5_glu_w, w_out, norm2_g, peer_wq, peer_subkeys,
           peer_u, peer_v, final_g):
    B, S, D = x.shape
    W, H = MLSTM_WIDTH, MLSTM_HEADS
    depth = norm1_g.shape[0]
    h = x.reshape(B * S, D)
    for l in range(depth):
        wl = w_in[l]
        n_gate = 2 * H
        w_main = jnp.concatenate([wl[:, :4 * W], wl[:, 4 * W + n_gate:]], axis=1).astype(BF16)
        w_gates = jnp.pad(wl[:, 4 * W:4 * W + n_gate], ((0, 0), (0, LANES - n_gate))).astype(BF16)
        bias_row = jnp.pad(b_gates[l].astype(F32), (0, LANES - n_gate)).reshape(1, LANES)
        s5w = _s5_weights(s5_lambda_re[l].astype(F32), s5_lambda_im[l].astype(F32), s5_log_dt[l],
                          s5_b_re[l].astype(F32), s5_b_im[l].astype(F32), s5_c_re[l].astype(F32),
                          s5_c_im[l].astype(F32), s5_d[l], s5_glu_w[l].astype(F32))
        wo = w_out[l].astype(BF16)

        z, gates = _inproj(h, norm1_g[l].reshape(1, D).astype(F32), w_main, w_gates)
        hm = _mlstm(z, gates, bias_row, conv_qk_w[l].astype(F32),
                    mlstm_norm_g[l].reshape(1, W).astype(F32), B, S)
        ys = _s5(z, *s5w, B, S)
        h1, xn2 = _outproj(hm, ys, wo[:W], wo[W:], h, norm2_g[l].reshape(1, D).astype(F32))

        i1, i2, gate = _route(xn2, peer_wq[l].astype(BF16), peer_subkeys[l].astype(BF16))
        g3 = _gates(i1, i2, gate)
        assert l == depth - 1, "kernel is specialised to a single layer followed by the final norm"
        h = _experts(xn2, peer_u[l].astype(BF16), peer_v[l].astype(BF16), g3, h1,
                     final_g.reshape(1, D).astype(F32))
    return h.reshape(B, S, D)
```

```python
import functools
import math

import jax
import jax.numpy as jnp
import numpy as np
from jax import lax
from jax.experimental import pallas as pl
from jax.experimental.pallas import tpu as pltpu

F32 = jnp.float32
BF16 = jnp.bfloat16

D_MODEL = 2048
MLSTM_WIDTH = 1024
MLSTM_HEADS = 4
HEAD_DIM = 256
CHUNK = 128
CONV_WIDTH = 4
S5_WIDTH = 1024
S5_GROUP = 16
S5_GROUPS = 64
S5_STATE = 64
PEER_HEADS = 8
PEER_NKEYS = 128
PEER_EXPERTS = PEER_NKEYS * PEER_NKEYS
PEER_TOPK = 16
PEER_QDIM = 256
RMS_EPS = 1e-6

LANES = 128
SUBLANES = 8
VMEM_LIMIT = 56 * 1024 * 1024

NT_DIMS = (((1,), (1,)), ((), ()))


def _params(*sem):
    return pltpu.CompilerParams(dimension_semantics=sem, vmem_limit_bytes=VMEM_LIMIT)


def _rms_scale(v):
    return lax.rsqrt(jnp.mean(v * v, axis=-1, keepdims=True) + RMS_EPS)


def _gelu(v):
    return 0.5 * v * (1.0 + lax.erf(v * np.float32(math.sqrt(0.5))))


def _sigmoid(v):
    return 1.0 / (1.0 + jnp.exp(-v))


def _log_sigmoid(v):
    return jnp.minimum(v, 0.0) - jnp.log1p(jnp.exp(-jnp.abs(v)))


def _inproj_kernel(x_ref, g_ref, w_ref, wg_ref, z_ref, gates_ref, xn_ref):
    @pl.when(pl.program_id(1) == 0)
    def _():
        x = x_ref[...]
        xn_ref[...] = (x * _rms_scale(x) * g_ref[...]).astype(BF16)
        gates_ref[...] = jnp.dot(xn_ref[...], wg_ref[...], preferred_element_type=F32)

    z_ref[...] = jnp.dot(xn_ref[...], w_ref[...], preferred_element_type=F32)


def _inproj(x2d, g_row, w_main, w_gates, tm=1024, tn=1024):
    T, D = x2d.shape
    N = w_main.shape[1]
    return pl.pallas_call(
        _inproj_kernel,
        grid=(T // tm, N // tn),
        in_specs=[
            pl.BlockSpec((tm, D), lambda i, j: (i, 0)),
            pl.BlockSpec((1, D), lambda i, j: (0, 0)),
            pl.BlockSpec((D, tn), lambda i, j: (0, j)),
            pl.BlockSpec((D, LANES), lambda i, j: (0, 0)),
        ],
        out_specs=[
            pl.BlockSpec((tm, tn), lambda i, j: (i, j)),
            pl.BlockSpec((tm, LANES), lambda i, j: (i, 0)),
        ],
        out_shape=[jax.ShapeDtypeStruct((T, N), F32), jax.ShapeDtypeStruct((T, LANES), F32)],
        scratch_shapes=[pltpu.VMEM((tm, D), BF16)],
        compiler_params=_params("parallel", "arbitrary"),
        name="inproj",
    )(x2d, g_row, w_main, w_gates)


def _mlstm_kernel(q_ref, k_ref, v_ref, o_ref, gates_ref, bias_ref, convw_ref, ng_ref, hm_ref,
                  xe_ref, c_ref, n_ref, m_ref):
    L, H, Dh, W = CHUNK, MLSTM_HEADS, HEAD_DIM, MLSTM_WIDTH
    tail = SUBLANES

    @pl.when(pl.program_id(1) == 0)
    def _():
        xe_ref[0:tail, :] = jnp.zeros((tail, 2 * W), F32)
        c_ref[...] = jnp.zeros_like(c_ref)
        n_ref[...] = jnp.zeros_like(n_ref)
        m_ref[...] = jnp.zeros_like(m_ref)

    xe_ref[tail:tail + L, 0:W] = q_ref[...]
    xe_ref[tail:tail + L, W:2 * W] = k_ref[...]
    conv = jnp.zeros((L, 2 * W), F32)
    for j in range(CONV_WIDTH):
        off = tail - (CONV_WIDTH - 1) + j
        conv = conv + convw_ref[j:j + 1, :] * xe_ref[off:off + L, :]
    xe_ref[0:tail, :] = xe_ref[L:L + tail, :]
    qk = conv * _sigmoid(conv)

    g = gates_ref[...] + bias_ref[...]
    g_t = g.T
    row = lax.broadcasted_iota(jnp.int32, (L, L), 0)
    col = lax.broadcasted_iota(jnp.int32, (L, L), 1)
    causal = col <= row
    tri = jnp.where(causal, 1.0, 0.0).astype(F32)
    tri_t = jnp.where(row <= col, 1.0, 0.0).astype(F32)
    hi = lax.Precision.HIGHEST
    a_cols = jnp.dot(tri, _log_sigmoid(g), precision=hi, preferred_element_type=F32)
    a_rows = jnp.dot(_log_sigmoid(g_t), tri_t, precision=hi, preferred_element_type=F32)

    scale = np.float32(Dh ** -0.5)
    for h in range(H):
        a_c = a_cols[:, H + h:H + h + 1]
        a_r = a_rows[H + h:H + h + 1, :]
        li_c = g[:, h:h + 1]
        li_r = g_t[h:h + 1, :]
        a_tot = a_r[:, L - 1:L]
        m0 = m_ref[h]
        c0 = c_ref[h]
        n0 = n_ref[h]

        q = qk[:, h * Dh:(h + 1) * Dh] * scale
        k = qk[:, W + h * Dh:W + (h + 1) * Dh]
        v = v_ref[:, h * Dh:(h + 1) * Dh]
        qb, kb, vb = q.astype(BF16), k.astype(BF16), v.astype(BF16)

        dmat = jnp.where(causal, a_c - a_r + li_r, -jnp.inf)
        m_inter = a_c + m0
        m_t = jnp.maximum(m_inter, jnp.max(dmat, axis=-1, keepdims=True))
        wm = jnp.exp(dmat - m_t)
        s_inter = jnp.exp(m_inter - m_t)
        s = lax.dot_general(qb, kb, NT_DIMS, preferred_element_type=F32) * wm
        num = (jnp.dot(s.astype(BF16), vb, preferred_element_type=F32)
               + s_inter * jnp.dot(qb, c0.astype(BF16), preferred_element_type=F32))
        den = (jnp.sum(s, axis=-1, keepdims=True)
               + s_inter * jnp.sum(q * n0, axis=-1, keepdims=True))
        hh = num / jnp.maximum(jnp.abs(den), jnp.exp(-m_t))
        hh = hh * _rms_scale(hh)
        og = _sigmoid(o_ref[:, h * Dh:(h + 1) * Dh])
        hm_ref[:, h * Dh:(h + 1) * Dh] = (hh * ng_ref[:, h * Dh:(h + 1) * Dh] * og).astype(BF16)

        g_r = a_tot - a_r + li_r
        g_c = a_tot - a_c + li_c
        m_loc = jnp.max(g_r, axis=-1, keepdims=True)
        kw = k * jnp.exp(g_c - m_loc)
        d_c = jnp.dot(kw.T.astype(BF16), vb, preferred_element_type=F32)
        d_n = jnp.sum(kw, axis=0, keepdims=True)
        m_new = jnp.maximum(a_tot + m0, m_loc)
        s_old = jnp.exp(a_tot + m0 - m_new)
        s_new = jnp.exp(m_loc - m_new)
        c_ref[h] = s_old * c0 + s_new * d_c
        n_ref[h] = s_old * n0 + s_new * d_n
        m_ref[h] = m_new


def _mlstm(z, gates, bias_row, conv_w, norm_g_row, batch, seq):
    T = z.shape[0]
    L, W = CHUNK, MLSTM_WIDTH
    nc = seq // L
    blk = lambda col: pl.BlockSpec((L, W), lambda b, c, col=col: (b * nc + c, col))
    return pl.pallas_call(
        _mlstm_kernel,
        grid=(batch, nc),
        in_specs=[
            blk(0), blk(1), blk(2), blk(3),
            pl.BlockSpec((L, LANES), lambda b, c: (b * nc + c, 0)),
            pl.BlockSpec((1, LANES), lambda b, c: (0, 0)),
            pl.BlockSpec((CONV_WIDTH, 2 * W), lambda b, c: (0, 0)),
            pl.BlockSpec((1, W), lambda b, c: (0, 0)),
        ],
        out_specs=pl.BlockSpec((L, W), lambda b, c: (b * nc + c, 0)),
        out_shape=jax.ShapeDtypeStruct((T, W), BF16),
        scratch_shapes=[
            pltpu.VMEM((L + 2 * SUBLANES, 2 * W), F32),
            pltpu.VMEM((MLSTM_HEADS, HEAD_DIM, HEAD_DIM), F32),
            pltpu.VMEM((MLSTM_HEADS, 1, HEAD_DIM), F32),
            pltpu.VMEM((MLSTM_HEADS, 1, 1), F32),
        ],
        compiler_params=_params("parallel", "arbitrary"),
        name="mlstm",
    )(z, z, z, z, gates, bias_row, conv_w, norm_g_row)


S5_BLOCK = 256
S5_SEG = S5_BLOCK // SUBLANES
S5_TILES = S5_WIDTH // LANES
S5_TILE_STATE = 8 * S5_STATE


def _s5_kernel(u_ref, win_ref, wout_ref, wglu_ref, a_ref, aseg_ref, d_ref, ys_ref,
               up_ref, x_ref, xs_ref, yp_ref, carry_ref):
    nseg, seg, half = SUBLANES, S5_SEG, S5_TILE_STATE

    @pl.when(pl.program_id(2) == 0)
    def _():
        carry_ref[...] = jnp.zeros_like(carry_ref)

    for i in range(seg):
        up_ref[nseg * i:nseg * (i + 1), :] = u_ref[pl.ds(i, nseg, stride=seg), :]

    def cmul_add(ar, ai, sr, si, xr, xi):
        return ar * sr - ai * si + xr, ar * si + ai * sr + xi

    u_t = up_ref[...]
    x_ref[...] = jnp.dot(u_t.astype(BF16), win_ref[...], preferred_element_type=F32)
    ar = a_ref[:, 0:half]
    ai = a_ref[:, half:2 * half]

    fr = jnp.zeros((nseg, half), F32)
    fi = jnp.zeros((nseg, half), F32)
    for i in range(seg):
        rows = slice(nseg * i, nseg * (i + 1))
        fr, fi = cmul_add(ar, ai, fr, fi, x_ref[rows, 0:half], x_ref[rows, half:2 * half])

    lr = aseg_ref[:, 0:half]
    lim = aseg_ref[:, half:2 * half]
    cr = carry_ref[:, 0:half]
    ci = carry_ref[:, half:2 * half]
    crs, cis = [], []
    for r in range(nseg):
        crs.append(cr)
        cis.append(ci)
        cr, ci = cmul_add(lr, lim, cr, ci, fr[r:r + 1, :], fi[r:r + 1, :])
    carry_ref[:, 0:half] = cr
    carry_ref[:, half:2 * half] = ci
    sr = jnp.concatenate(crs, axis=0)
    si = jnp.concatenate(cis, axis=0)

    for i in range(seg):
        rows = slice(nseg * i, nseg * (i + 1))
        sr, si = cmul_add(ar, ai, sr, si, x_ref[rows, 0:half], x_ref[rows, half:2 * half])
        xs_ref[rows, 0:half] = sr
        xs_ref[rows, half:2 * half] = si

    y = jnp.dot(xs_ref[...].astype(BF16), wout_ref[...], preferred_element_type=F32)
    y = y + d_ref[...] * u_t
    ab = jnp.dot(_gelu(y).astype(BF16), wglu_ref[...], preferred_element_type=F32)
    yp_ref[...] = ab[:, 0:LANES] * _sigmoid(ab[:, LANES:2 * LANES])

    for i in range(seg):
        ys_ref[pl.ds(i, nseg, stride=seg), :] = yp_ref[nseg * i:nseg * (i + 1), :]


def _s5(z, win, wout, wglu, a_rows, aseg_rows, d_row, batch, seq):
    T = z.shape[0]
    blk = S5_BLOCK
    nb = seq // blk
    u_col = (4 * MLSTM_WIDTH) // LANES
    tile = lambda shape: pl.BlockSpec((None,) + shape, lambda b, j, c: (j,) + (0,) * len(shape))
    return pl.pallas_call(
        _s5_kernel,
        grid=(batch, S5_TILES, nb),
        in_specs=[
            pl.BlockSpec((blk, LANES), lambda b, j, c: (b * nb + c, u_col + j)),
            tile(win.shape[1:]), tile(wout.shape[1:]), tile(wglu.shape[1:]),
            tile(a_rows.shape[1:]), tile(aseg_rows.shape[1:]),
            pl.BlockSpec((1, LANES), lambda b, j, c: (0, j)),
        ],
        out_specs=pl.BlockSpec((blk, LANES), lambda b, j, c: (b * nb + c, j)),
        out_shape=jax.ShapeDtypeStruct((T, S5_WIDTH), F32),
        scratch_shapes=[
            pltpu.VMEM((blk, LANES), F32),
            pltpu.VMEM((blk, 2 * S5_TILE_STATE), F32),
            pltpu.VMEM((blk, 2 * S5_TILE_STATE), F32),
            pltpu.VMEM((blk, LANES), F32),
            pltpu.VMEM((1, 2 * S5_TILE_STATE), F32),
        ],
        compiler_params=_params("parallel", "parallel", "arbitrary"),
        name="s5",
    )(z, win, wout, wglu, a_rows, aseg_rows, d_row)


def _s5_weights(lam_re, lam_im, log_dt, b_re, b_im, c_re, c_im, d, glu_w):
    G, P, Hg = S5_GROUPS, S5_STATE, S5_GROUP
    nt, gl = S5_TILES, G // S5_TILES
    dt = jnp.exp(log_dt.astype(F32))[:, None]
    mag = jnp.exp(lam_re * dt)
    ar = mag * jnp.cos(lam_im * dt)
    ai = mag * jnp.sin(lam_im * dt)
    den = lam_re * lam_re + lam_im * lam_im
    fr = ((ar - 1.0) * lam_re + ai * lam_im) / den
    fi = (ai * lam_re - (ar - 1.0) * lam_im) / den
    bbr = fr[..., None] * b_re - fi[..., None] * b_im
    bbi = fr[..., None] * b_im + fi[..., None] * b_re
    eye = jnp.eye(gl, dtype=F32)

    def tiles(v):
        return v.reshape((nt, gl) + v.shape[1:])

    bb = jnp.stack([tiles(bbr), tiles(bbi)], axis=2)
    win = jnp.einsum('tgrph,gk->tghrkp', bb, eye).reshape(nt, gl * Hg, 2 * gl * P)
    cc = jnp.stack([tiles(c_re), -tiles(c_im)], axis=2)
    wout = jnp.einsum('tgrhp,gk->trkpgh', cc, eye).reshape(nt, 2 * gl * P, gl * Hg)
    gw = tiles(glu_w).reshape(nt, gl, Hg, 2, Hg)
    wglu = jnp.einsum('tghrk,gq->tqhrgk', gw, eye).reshape(nt, gl * Hg, 2 * gl * Hg)

    def rows(re, im):
        return jnp.concatenate([tiles(re).reshape(nt, 1, gl * P), tiles(im).reshape(nt, 1, gl * P)], axis=-1)

    a_rows = jnp.broadcast_to(rows(ar, ai), (nt, SUBLANES, 2 * gl * P))
    pr, pi = ar, ai
    for _ in range(int(math.log2(S5_SEG))):
        pr, pi = pr * pr - pi * pi, 2.0 * pr * pi
    aseg_rows = rows(pr, pi)
    d_row = d.reshape(1, G * Hg).astype(F32)
    return win.astype(BF16), wout.astype(BF16), wglu.astype(BF16), a_rows, aseg_rows, d_row


def _outproj_kernel(hm_ref, ys_ref, w1_ref, w2_ref, x_ref, g_ref, h1_ref, xn_ref):
    mix = (jnp.dot(hm_ref[...], w1_ref[...], preferred_element_type=F32)
           + jnp.dot(ys_ref[...].astype(BF16), w2_ref[...], preferred_element_type=F32))
    h1 = x_ref[...] + mix
    h1_ref[...] = h1
    xn_ref[...] = (h1 * _rms_scale(h1) * g_ref[...]).astype(BF16)


def _outproj(hm, ys, w1, w2, x2d, g_row, tm=512):
    T, D = x2d.shape
    W = hm.shape[1]
    row = lambda w: pl.BlockSpec((tm, w), lambda i: (i, 0))
    full = lambda shape: pl.BlockSpec(shape, lambda i: (0,) * len(shape))
    return pl.pallas_call(
        _outproj_kernel,
        grid=(T // tm,),
        in_specs=[row(W), row(W), full(w1.shape), full(w2.shape), row(D), full((1, D))],
        out_specs=[row(D), row(D)],
        out_shape=[jax.ShapeDtypeStruct((T, D), F32), jax.ShapeDtypeStruct((T, D), BF16)],
        compiler_params=_params("parallel"),
        name="outproj",
    )(hm, ys, w1, w2, x2d, g_row)


ROUTE_TOKENS = SUBLANES * LANES
ROUTE_CHUNKS = ROUTE_TOKENS // LANES


def _sort_pairs(n):
    pairs = []
    p = 1
    while p < n:
        k = p
        while k >= 1:
            for j in range(k % p, n - k, 2 * k):
                for i in range(min(k, n - j - k)):
                    if (i + j) // (2 * p) == (i + j + k) // (2 * p):
                        pairs.append((i + j, i + j + k))
            k //= 2
        p *= 2
    return pairs


def _first_second(a, b):
    (av, ai), (bv, bi) = a, b
    a_first = (av > bv) | ((av == bv) & (ai < bi))
    return ((jnp.maximum(av, bv), jnp.where(a_first, ai, bi)),
            (jnp.minimum(av, bv), jnp.where(a_first, bi, ai)))


def _sorted_group(items):
    items = list(items)
    for i, j in _sort_pairs(len(items)):
        items[i], items[j] = _first_second(items[i], items[j])
    return items


def _merge_top(a, b):
    n = len(a)
    c = [_first_second(a[v], b[n - 1 - v])[0] for v in range(n)]
    j = n // 2
    while j >= 1:
        for i in range(n):
            if (i & j) == 0:
                c[i], c[i + j] = _first_second(c[i], c[i + j])
        j //= 2
    return c


def _top_sorted(groups):
    groups = list(groups)
    while len(groups) > 1:
        groups = [_merge_top(groups[i], groups[i + 1]) for i in range(0, len(groups), 2)]
    return groups[0]


def _product_top(top1, top2):
    K = PEER_TOPK
    pairs = [(r1, r2) for r1 in range(K) for r2 in range(K) if (r1 + 1) * (r2 + 1) <= K]
    cands = [(top1[r1][0] + top2[r2][0], jnp.full_like(top1[0][0], float(r1 * K + r2))) for r1, r2 in pairs]
    pad = (jnp.full_like(top1[0][0], -jnp.inf), jnp.full_like(top1[0][0], float(K * K)))
    cands += [pad] * (-len(cands) % K)
    groups = [cands[:K]] + [_sorted_group(cands[g:g + K]) for g in range(K, len(cands), K)]
    best = _top_sorted(groups)
    top_s = [v for v, _ in best]
    e = [jnp.exp(v - top_s[0]) for v in top_s]
    z = functools.reduce(lambda x, y: x + y, e)
    gates, i1s, i2s = [], [], []
    for r in range(K):
        pos = best[r][1]
        r1 = jnp.floor(pos * np.float32(1.0 / K))
        r2 = pos - r1 * np.float32(K)
        i1, i2 = top1[0][1], top2[0][1]
        for a in range(1, K):
            i1 = jnp.where(r1 == float(a), top1[a][1], i1)
            i2 = jnp.where(r2 == float(a), top2[a][1], i2)
        gates.append(e[r] / z)
        i1s.append(i1)
        i2s.append(i2)
    return i1s, i2s, gates


def _route_kernel(x_ref, wq_ref, sk_ref, i1_ref, i2_ref, gate_ref, q_ref, sc_ref):
    K, n, half = PEER_TOPK, PEER_NKEYS, PEER_QDIM // 2
    h = pl.program_id(1)

    @pl.when(h == 0)
    def _():
        q = jnp.dot(x_ref[...], wq_ref[...], preferred_element_type=F32)
        for hh in range(PEER_HEADS):
            q_ref[hh] = q[:, hh * PEER_QDIM:(hh + 1) * PEER_QDIM].astype(BF16)

    tops = []
    for p in range(2):
        for c in range(ROUTE_CHUNKS):
            qc = q_ref[h, c * LANES:(c + 1) * LANES, p * half:(p + 1) * half]
            sc_ref[pl.ds(c, n, stride=ROUTE_CHUNKS), :] = lax.dot_general(
                sk_ref[p], qc, NT_DIMS, preferred_element_type=F32)
        items = [(sc_ref[k * SUBLANES:(k + 1) * SUBLANES, :], jnp.full((SUBLANES, LANES), float(k), F32))
                 for k in range(n)]
        tops.append(_top_sorted([_sorted_group(items[g:g + K]) for g in range(0, n, K)]))

    i1s, i2s, gates = _product_top(*tops)
    for ref, vals in ((i1_ref, i1s), (i2_ref, i2s), (gate_ref, gates)):
        rows = ref.reshape(ROUTE_CHUNKS * K, LANES)
        for r in range(K):
            rows[pl.ds(r, ROUTE_CHUNKS, stride=K), :] = vals[r]


def _route(xn2, wq, subkeys):
    T, D = xn2.shape
    tt, K = ROUTE_TOKENS, PEER_TOPK
    out = pl.BlockSpec((ROUTE_CHUNKS, K, LANES), lambda i, h: (i, h, 0))
    shape = jax.ShapeDtypeStruct((T // LANES, PEER_HEADS * K, LANES), F32)
    return pl.pallas_call(
        _route_kernel,
        grid=(T // tt, PEER_HEADS),
        in_specs=[
            pl.BlockSpec((tt, D), lambda i, h: (i, 0)),
            pl.BlockSpec(wq.shape, lambda i, h: (0, 0)),
            pl.BlockSpec(subkeys.shape, lambda i, h: (0, 0, 0)),
        ],
        out_specs=[out, out, out],
        out_shape=[shape, shape, shape],
        scratch_shapes=[pltpu.VMEM((PEER_HEADS, tt, PEER_QDIM), BF16),
                        pltpu.VMEM((PEER_NKEYS * ROUTE_CHUNKS, LANES), F32)],
        compiler_params=_params("parallel", "arbitrary"),
        name="peer_route",
    )(xn2, wq, subkeys)


GATE_TOKENS = LANES
GATE_UNROLL = 32


def _gates_kernel(i1_ref, i2_ref, gate_ref, g_ref, a_ref, b_ref, w_ref):
    n = PEER_NKEYS
    a_ref[...] = i1_ref[...].T
    b_ref[...] = i2_ref[...].T
    w_ref[...] = gate_ref[...].T
    keys = lax.broadcasted_iota(jnp.int32, (n, n), 0).astype(F32)

    def body(step, carry):
        base = pl.multiple_of(step * GATE_UNROLL, GATE_UNROLL)
        for u in range(GATE_UNROLL):
            t = base + u
            r1 = a_ref[pl.ds(t, 1), :]
            r2 = b_ref[pl.ds(t, 1), :]
            w = w_ref[pl.ds(t, 1), :]
            at = jnp.where(keys == r1, 1.0, 0.0).astype(BF16)
            bt = jnp.where(keys == r2, w, 0.0).astype(BF16)
            g_ref[t] = lax.dot_general(at, bt, NT_DIMS, preferred_element_type=F32)
        return carry

    lax.fori_loop(0, GATE_TOKENS // GATE_UNROLL, body, 0)


def _gates(i1, i2, gate):
    nchunk, slots, tt = i1.shape
    n = PEER_NKEYS
    spec = pl.BlockSpec((None, slots, tt), lambda i: (i, 0, 0))
    return pl.pallas_call(
        _gates_kernel,
        grid=(nchunk,),
        in_specs=[spec, spec, spec],
        out_specs=pl.BlockSpec((tt, n, n), lambda i: (i, 0, 0)),
        out_shape=jax.ShapeDtypeStruct((nchunk * tt, n, n), F32),
        scratch_shapes=[pltpu.VMEM((tt, slots), F32)] * 3,
        compiler_params=_params("parallel"),
        name="peer_gates",
    )(i1, i2, gate)


EXPERT_G_ROWS = SUBLANES


def _experts_kernel(x_ref, u_ref, v_ref, g_ref, h1_ref, fg_ref, y_ref):
    e, k = pl.program_id(1), pl.program_id(2)
    first = jnp.logical_and(e == 0, k == 0)
    last = jnp.logical_and(e == pl.num_programs(1) - 1, k == pl.num_programs(2) - 1)

    @pl.when(first)
    def _():
        y_ref[...] = jnp.zeros_like(y_ref)

    u = u_ref[...].astype(BF16)
    v = v_ref[...].astype(BF16)
    act = _gelu(lax.dot_general(x_ref[...], u, NT_DIMS, preferred_element_type=F32))
    n = PEER_NKEYS
    tm, rows, _ = g_ref.shape
    nc = u_ref.shape[0] // n
    g_rows = g_ref.reshape(tm * rows, n)
    w = jnp.concatenate([(g_rows[pl.ds(k * nc + c, tm, stride=rows), :] * act[:, c * n:(c + 1) * n]).astype(BF16)
                         for c in range(nc)], axis=1)
    y_ref[...] += jnp.dot(w, v, preferred_element_type=F32)

    @pl.when(last)
    def _():
        h2 = h1_ref[...] + y_ref[...]
        y_ref[...] = h2 * _rms_scale(h2) * fg_ref[...]


def _experts(xn2, u_tab, v_tab, g3, h1, fg_row, tm=1024, te=512):
    T, D = xn2.shape
    E = u_tab.shape[0]
    n = PEER_NKEYS
    sub = EXPERT_G_ROWS * n // te
    once = pl.Buffered(1)
    return pl.pallas_call(
        _experts_kernel,
        grid=(T // tm, E // (te * sub), sub),
        in_specs=[
            pl.BlockSpec((tm, D), lambda i, e, k: (i, 0), pipeline_mode=once),
            pl.BlockSpec((te, D), lambda i, e, k: (e * sub + k, 0)),
            pl.BlockSpec((te, D), lambda i, e, k: (e * sub + k, 0)),
            pl.BlockSpec((tm, EXPERT_G_ROWS, n), lambda i, e, k: (i, e, 0)),
            pl.BlockSpec((tm, D), lambda i, e, k: (i, 0), pipeline_mode=once),
            pl.BlockSpec((1, D), lambda i, e, k: (0, 0)),
        ],
        out_specs=pl.BlockSpec((tm, D), lambda i, e, k: (i, 0), pipeline_mode=once),
        out_shape=jax.ShapeDtypeStruct((T, D), F32),
        compiler_params=_params("parallel", "arbitrary", "arbitrary"),
        name="peer_experts",
    )(xn2, u_tab, v_tab, g3, h1, fg_row)


def kernel(x, norm1_g, w_in, b_gates, conv_qk_w, mlstm_norm_g, s5_lambda_re, s5_lambda_im, s5_log_dt,
           s5_b_re, s5_b_im, s5_c_re, s5_c_im, s5_d, s5_glu_w, w_out, norm2_g, peer_wq, peer_subkeys,
           peer_u, peer_v, final_g):
    B, S, D = x.shape
    W, H = MLSTM_WIDTH, MLSTM_HEADS
    depth = norm1_g.shape[0]
    h = x.reshape(B * S, D)
    for l in range(depth):
        wl = w_in[l]
        n_gate = 2 * H
        w_main = jnp.concatenate([wl[:, :4 * W], wl[:, 4 * W + n_gate:]], axis=1).astype(BF16)
        w_gates = jnp.pad(wl[:, 4 * W:4 * W + n_gate], ((0, 0), (0, LANES - n_gate))).astype(BF16)
        bias_row = jnp.pad(b_gates[l].astype(F32), (0, LANES - n_gate)).reshape(1, LANES)
        s5w = _s5_weights(s5_lambda_re[l].astype(F32), s5_lambda_im[l].astype(F32), s5_log_dt[l],
                          s5_b_re[l].astype(F32), s5_b_im[l].astype(F32), s5_c_re[l].astype(F32),
                          s5_c_im[l].astype(F32), s5_d[l], s5_glu_w[l].astype(F32))
        wo = w_out[l].astype(BF16)

        z, gates = _inproj(h, norm1_g[l].reshape(1, D).astype(F32), w_main, w_gates)
        hm = _mlstm(z, gates, bias_row, conv_qk_w[l].astype(F32),
                    mlstm_norm_g[l].reshape(1, W).astype(F32), B, S)
        ys = _s5(z, *s5w, B, S)
        h1, xn2 = _outproj(hm, ys, wo[:W], wo[W:], h, norm2_g[l].reshape(1, D).astype(F32))

        i1, i2, gate = _route(xn2, peer_wq[l].astype(BF16), peer_subkeys[l].astype(BF16))
        g3 = _gates(i1, i2, gate)
        assert l == depth - 1, "kernel is specialised to a single layer followed by the final norm"
        h = _experts(xn2, peer_u[l], peer_v[l], g3, h1,
                     final_g.reshape(1, D).astype(F32))
    return h.reshape(B, S, D)
```

```python
import functools
import math

import jax
import jax.numpy as jnp
import numpy as np
from jax import lax
from jax.experimental import pallas as pl
from jax.experimental.pallas import tpu as pltpu

F32 = jnp.float32
BF16 = jnp.bfloat16

D_MODEL = 2048
MLSTM_WIDTH = 1024
MLSTM_HEADS = 4
HEAD_DIM = 256
CHUNK = 128
CONV_WIDTH = 4
S5_WIDTH = 1024
S5_GROUP = 16
S5_GROUPS = 64
S5_STATE = 64
PEER_HEADS = 8
PEER_NKEYS = 128
PEER_EXPERTS = PEER_NKEYS * PEER_NKEYS
PEER_TOPK = 16
PEER_QDIM = 256
RMS_EPS = 1e-6

LANES = 128
SUBLANES = 8
VMEM_LIMIT = 56 * 1024 * 1024

NT_DIMS = (((1,), (1,)), ((), ()))


def _params(*sem):
    return pltpu.CompilerParams(dimension_semantics=sem, vmem_limit_bytes=VMEM_LIMIT)


def _rms_scale(v):
    return lax.rsqrt(jnp.mean(v * v, axis=-1, keepdims=True) + RMS_EPS)


def _gelu(v):
    return 0.5 * v * (1.0 + lax.erf(v * np.float32(math.sqrt(0.5))))


def _sigmoid(v):
    return 1.0 / (1.0 + jnp.exp(-v))


def _log_sigmoid(v):
    return jnp.minimum(v, 0.0) - jnp.log1p(jnp.exp(-jnp.abs(v)))


def _inproj_kernel(x_ref, g_ref, w_ref, wu_ref, wg_ref, z_ref, gates_ref, xn_ref):
    j = pl.program_id(1)
    last = pl.num_programs(1) - 1

    @pl.when(j == 0)
    def _():
        x = x_ref[...]
        xn_ref[...] = (x * _rms_scale(x) * g_ref[...]).astype(BF16)
        gates_ref[...] = jnp.dot(xn_ref[...], wg_ref[...], preferred_element_type=F32)

    @pl.when(j < last)
    def _():
        z_ref[...] = jnp.dot(xn_ref[...], w_ref[...], preferred_element_type=F32)

    @pl.when(j == last)
    def _():
        z_ref[...] = jnp.dot(xn_ref[...], wu_ref[...], preferred_element_type=F32)


def _inproj(x2d, g_row, w_all, w_u, w_gates, tm=1024, tn=1024):
    T, D = x2d.shape
    n_main = (4 * MLSTM_WIDTH) // tn
    assert w_u.shape == (D, tn)
    return pl.pallas_call(
        _inproj_kernel,
        grid=(T // tm, n_main + 1),
        in_specs=[
            pl.BlockSpec((tm, D), lambda i, j: (i, 0)),
            pl.BlockSpec((1, D), lambda i, j: (0, 0)),
            pl.BlockSpec((D, tn), lambda i, j: (0, jnp.minimum(j, n_main - 1))),
            pl.BlockSpec((D, tn), lambda i, j: (0, 0)),
            pl.BlockSpec((D, LANES), lambda i, j: (0, 0)),
        ],
        out_specs=[
            pl.BlockSpec((tm, tn), lambda i, j: (i, j)),
            pl.BlockSpec((tm, LANES), lambda i, j: (i, 0)),
        ],
        out_shape=[jax.ShapeDtypeStruct((T, (n_main + 1) * tn), F32), jax.ShapeDtypeStruct((T, LANES), F32)],
        scratch_shapes=[pltpu.VMEM((tm, D), BF16)],
        compiler_params=_params("parallel", "arbitrary"),
        name="inproj",
    )(x2d, g_row, w_all, w_u, w_gates)


def _mlstm_kernel(q_ref, k_ref, v_ref, o_ref, gates_ref, bias_ref, convw_ref, ng_ref, hm_ref,
                  xe_ref, c_ref, n_ref, m_ref):
    L, H, Dh, W = CHUNK, MLSTM_HEADS, HEAD_DIM, MLSTM_WIDTH
    tail = SUBLANES

    @pl.when(pl.program_id(1) == 0)
    def _():
        xe_ref[0:tail, :] = jnp.zeros((tail, 2 * W), F32)
        c_ref[...] = jnp.zeros_like(c_ref)
        n_ref[...] = jnp.zeros_like(n_ref)
        m_ref[...] = jnp.zeros_like(m_ref)

    xe_ref[tail:tail + L, 0:W] = q_ref[...]
    xe_ref[tail:tail + L, W:2 * W] = k_ref[...]
    conv = jnp.zeros((L, 2 * W), F32)
    for j in range(CONV_WIDTH):
        off = tail - (CONV_WIDTH - 1) + j
        conv = conv + convw_ref[j:j + 1, :] * xe_ref[off:off + L, :]
    xe_ref[0:tail, :] = xe_ref[L:L + tail, :]
    qk = conv * _sigmoid(conv)

    g = gates_ref[...] + bias_ref[...]
    g_t = g.T
    row = lax.broadcasted_iota(jnp.int32, (L, L), 0)
    col = lax.broadcasted_iota(jnp.int32, (L, L), 1)
    causal = col <= row
    tri = jnp.where(causal, 1.0, 0.0).astype(F32)
    tri_t = jnp.where(row <= col, 1.0, 0.0).astype(F32)
    hi = lax.Precision.HIGHEST
    a_cols = jnp.dot(tri, _log_sigmoid(g), precision=hi, preferred_element_type=F32)
    a_rows = jnp.dot(_log_sigmoid(g_t), tri_t, precision=hi, preferred_element_type=F32)

    scale = np.float32(Dh ** -0.5)
    for h in range(H):
        a_c = a_cols[:, H + h:H + h + 1]
        a_r = a_rows[H + h:H + h + 1, :]
        li_c = g[:, h:h + 1]
        li_r = g_t[h:h + 1, :]
        a_tot = a_r[:, L - 1:L]
        m0 = m_ref[h]
        c0 = c_ref[h]
        n0 = n_ref[h]

        q = qk[:, h * Dh:(h + 1) * Dh] * scale
        k = qk[:, W + h * Dh:W + (h + 1) * Dh]
        v = v_ref[:, h * Dh:(h + 1) * Dh]
        qb, kb, vb = q.astype(BF16), k.astype(BF16), v.astype(BF16)

        dmat = jnp.where(causal, a_c - a_r + li_r, -jnp.inf)
        m_inter = a_c + m0
        m_t = jnp.maximum(m_inter, jnp.max(dmat, axis=-1, keepdims=True))
        wm = jnp.exp(dmat - m_t)
        s_inter = jnp.exp(m_inter - m_t)
        s = lax.dot_general(qb, kb, NT_DIMS, preferred_element_type=F32) * wm
        num = (jnp.dot(s.astype(BF16), vb, preferred_element_type=F32)
               + s_inter * jnp.dot(qb, c0.astype(BF16), preferred_element_type=F32))
        den = (jnp.sum(s, axis=-1, keepdims=True)
               + s_inter * jnp.sum(q * n0, axis=-1, keepdims=True))
        hh = num / jnp.maximum(jnp.abs(den), jnp.exp(-m_t))
        hh = hh * _rms_scale(hh)
        og = _sigmoid(o_ref[:, h * Dh:(h + 1) * Dh])
        hm_ref[:, h * Dh:(h + 1) * Dh] = (hh * ng_ref[:, h * Dh:(h + 1) * Dh] * og).astype(BF16)

        g_r = a_tot - a_r + li_r
        g_c = a_tot - a_c + li_c
        m_loc = jnp.max(g_r, axis=-1, keepdims=True)
        kw = k * jnp.exp(g_c - m_loc)
        d_c = jnp.dot(kw.T.astype(BF16), vb, preferred_element_type=F32)
        d_n = jnp.sum(kw, axis=0, keepdims=True)
        m_new = jnp.maximum(a_tot + m0, m_loc)
        s_old = jnp.exp(a_tot + m0 - m_new)
        s_new = jnp.exp(m_loc - m_new)
        c_ref[h] = s_old * c0 + s_new * d_c
        n_ref[h] = s_old * n0 + s_new * d_n
        m_ref[h] = m_new


def _mlstm(z, gates, bias_row, conv_w, norm_g_row, batch, seq):
    T = z.shape[0]
    L, W = CHUNK, MLSTM_WIDTH
    nc = seq // L
    blk = lambda col: pl.BlockSpec((L, W), lambda b, c, col=col: (b * nc + c, col))
    return pl.pallas_call(
        _mlstm_kernel,
        grid=(batch, nc),
        in_specs=[
            blk(0), blk(1), blk(2), blk(3),
            pl.BlockSpec((L, LANES), lambda b, c: (b * nc + c, 0)),
            pl.BlockSpec((1, LANES), lambda b, c: (0, 0)),
            pl.BlockSpec((CONV_WIDTH, 2 * W), lambda b, c: (0, 0)),
            pl.BlockSpec((1, W), lambda b, c: (0, 0)),
        ],
        out_specs=pl.BlockSpec((L, W), lambda b, c: (b * nc + c, 0)),
        out_shape=jax.ShapeDtypeStruct((T, W), BF16),
        scratch_shapes=[
            pltpu.VMEM((L + 2 * SUBLANES, 2 * W), F32),
            pltpu.VMEM((MLSTM_HEADS, HEAD_DIM, HEAD_DIM), F32),
            pltpu.VMEM((MLSTM_HEADS, 1, HEAD_DIM), F32),
            pltpu.VMEM((MLSTM_HEADS, 1, 1), F32),
        ],
        compiler_params=_params("parallel", "arbitrary"),
        name="mlstm",
    )(z, z, z, z, gates, bias_row, conv_w, norm_g_row)


S5_BLOCK = 256
S5_SEG = S5_BLOCK // SUBLANES
S5_TILES = S5_WIDTH // LANES
S5_TILE_STATE = 8 * S5_STATE


def _s5_kernel(u_ref, win_ref, wout_ref, wglu_ref, a_ref, aseg_ref, d_ref, ys_ref,
               up_ref, x_ref, xs_ref, yp_ref, carry_ref):
    nseg, seg, half = SUBLANES, S5_SEG, S5_TILE_STATE
    chains = range(u_ref.shape[0])

    @pl.when(pl.program_id(1) == 0)
    def _():
        carry_ref[...] = jnp.zeros_like(carry_ref)

    def cmul_add(ar, ai, sr, si, xr, xi):
        return ar * sr - ai * si + xr, ar * si + ai * sr + xi

    def x_rows(b, i):
        rows = slice(nseg * i, nseg * (i + 1))
        return x_ref[b, rows, 0:half], x_ref[b, rows, half:2 * half]

    for i in range(seg):
        for b in chains:
            up_ref[b, nseg * i:nseg * (i + 1), :] = u_ref[b, pl.ds(i, nseg, stride=seg), :]
    for b in chains:
        x_ref[b] = jnp.dot(up_ref[b].astype(BF16), win_ref[...], preferred_element_type=F32)
    ar = a_ref[:, 0:half]
    ai = a_ref[:, half:2 * half]

    fin = [(jnp.zeros((nseg, half), F32), jnp.zeros((nseg, half), F32)) for _ in chains]
    for i in range(seg):
        for b in chains:
            fin[b] = cmul_add(ar, ai, *fin[b], *x_rows(b, i))

    lr = aseg_ref[:, 0:half]
    lim = aseg_ref[:, half:2 * half]
    state = []
    for b in chains:
        cr = carry_ref[b, :, 0:half]
        ci = carry_ref[b, :, half:2 * half]
        crs, cis = [], []
        for r in range(nseg):
            crs.append(cr)
            cis.append(ci)
            cr, ci = cmul_add(lr, lim, cr, ci, fin[b][0][r:r + 1, :], fin[b][1][r:r + 1, :])
        carry_ref[b, :, 0:half] = cr
        carry_ref[b, :, half:2 * half] = ci
        state.append((jnp.concatenate(crs, axis=0), jnp.concatenate(cis, axis=0)))

    for i in range(seg):
        rows = slice(nseg * i, nseg * (i + 1))
        for b in chains:
            state[b] = cmul_add(ar, ai, *state[b], *x_rows(b, i))
            xs_ref[b, rows, 0:half] = state[b][0]
            xs_ref[b, rows, half:2 * half] = state[b][1]

    for b in chains:
        y = jnp.dot(xs_ref[b].astype(BF16), wout_ref[...], preferred_element_type=F32)
        y = y + d_ref[...] * up_ref[b]
        ab = jnp.dot(_gelu(y).astype(BF16), wglu_ref[...], preferred_element_type=F32)
        yp_ref[b] = ab[:, 0:LANES] * _sigmoid(ab[:, LANES:2 * LANES])
    for i in range(seg):
        for b in chains:
            ys_ref[b, pl.ds(i, nseg, stride=seg), :] = yp_ref[b, nseg * i:nseg * (i + 1), :]


def _s5(z, win, wout, wglu, a_rows, aseg_rows, d_row, batch, seq):
    T, N = z.shape
    blk = S5_BLOCK
    nb = seq // blk
    u_col = (4 * MLSTM_WIDTH) // LANES
    tile = lambda shape: pl.BlockSpec((None,) + shape, lambda j, c: (j,) + (0,) * len(shape))
    ys = pl.pallas_call(
        _s5_kernel,
        grid=(S5_TILES, nb),
        in_specs=[
            pl.BlockSpec((batch, blk, LANES), lambda j, c: (0, c, u_col + j)),
            tile(win.shape[1:]), tile(wout.shape[1:]), tile(wglu.shape[1:]),
            tile(a_rows.shape[1:]), tile(aseg_rows.shape[1:]),
            pl.BlockSpec((1, LANES), lambda j, c: (0, j)),
        ],
        out_specs=pl.BlockSpec((batch, blk, LANES), lambda j, c: (0, c, j)),
        out_shape=jax.ShapeDtypeStruct((batch, seq, S5_WIDTH), F32),
        scratch_shapes=[
            pltpu.VMEM((batch, blk, LANES), F32),
            pltpu.VMEM((batch, blk, 2 * S5_TILE_STATE), F32),
            pltpu.VMEM((batch, blk, 2 * S5_TILE_STATE), F32),
            pltpu.VMEM((batch, blk, LANES), F32),
            pltpu.VMEM((batch, 1, 2 * S5_TILE_STATE), F32),
        ],
        compiler_params=_params("parallel", "arbitrary"),
        name="s5",
    )(z.reshape(batch, seq, N), win, wout, wglu, a_rows, aseg_rows, d_row)
    return ys.reshape(T, S5_WIDTH)


def _s5_weights(lam_re, lam_im, log_dt, b_re, b_im, c_re, c_im, d, glu_w):
    G, P, Hg = S5_GROUPS, S5_STATE, S5_GROUP
    nt, gl = S5_TILES, G // S5_TILES
    dt = jnp.exp(log_dt.astype(F32))[:, None]
    mag = jnp.exp(lam_re * dt)
    ar = mag * jnp.cos(lam_im * dt)
    ai = mag * jnp.sin(lam_im * dt)
    den = lam_re * lam_re + lam_im * lam_im
    fr = ((ar - 1.0) * lam_re + ai * lam_im) / den
    fi = (ai * lam_re - (ar - 1.0) * lam_im) / den
    bbr = fr[..., None] * b_re - fi[..., None] * b_im
    bbi = fr[..., None] * b_im + fi[..., None] * b_re
    eye = jnp.eye(gl, dtype=F32)

    def tiles(v):
        return v.reshape((nt, gl) + v.shape[1:])

    bb = jnp.stack([tiles(bbr), tiles(bbi)], axis=2)
    win = jnp.einsum('tgrph,gk->tghrkp', bb, eye).reshape(nt, gl * Hg, 2 * gl * P)
    cc = jnp.stack([tiles(c_re), -tiles(c_im)], axis=2)
    wout = jnp.einsum('tgrhp,gk->trkpgh', cc, eye).reshape(nt, 2 * gl * P, gl * Hg)
    gw = tiles(glu_w).reshape(nt, gl, Hg, 2, Hg)
    wglu = jnp.einsum('tghrk,gq->tqhrgk', gw, eye).reshape(nt, gl * Hg, 2 * gl * Hg)

    def rows(re, im):
        return jnp.concatenate([tiles(re).reshape(nt, 1, gl * P), tiles(im).reshape(nt, 1, gl * P)], axis=-1)

    a_rows = jnp.broadcast_to(rows(ar, ai), (nt, SUBLANES, 2 * gl * P))
    pr, pi = ar, ai
    for _ in range(int(math.log2(S5_SEG))):
        pr, pi = pr * pr - pi * pi, 2.0 * pr * pi
    aseg_rows = rows(pr, pi)
    d_row = d.reshape(1, G * Hg).astype(F32)
    return win.astype(BF16), wout.astype(BF16), wglu.astype(BF16), a_rows, aseg_rows, d_row


def _outproj_kernel(hm_ref, ys_ref, w1_ref, w2_ref, x_ref, g_ref, h1_ref, xn_ref):
    mix = (jnp.dot(hm_ref[...], w1_ref[...], preferred_element_type=F32)
           + jnp.dot(ys_ref[...].astype(BF16), w2_ref[...], preferred_element_type=F32))
    h1 = x_ref[...] + mix
    h1_ref[...] = h1
    xn_ref[...] = (h1 * _rms_scale(h1) * g_ref[...]).astype(BF16)


def _outproj(hm, ys, w, x2d, g_row, tm=512):
    T, D = x2d.shape
    W = hm.shape[1]
    row = lambda w: pl.BlockSpec((tm, w), lambda i: (i, 0))
    half = lambda k: pl.BlockSpec((W, D), lambda i: (k, 0))
    return pl.pallas_call(
        _outproj_kernel,
        grid=(T // tm,),
        in_specs=[row(W), row(W), half(0), half(1), row(D), pl.BlockSpec((1, D), lambda i: (0, 0))],
        out_specs=[row(D), row(D)],
        out_shape=[jax.ShapeDtypeStruct((T, D), F32), jax.ShapeDtypeStruct((T, D), BF16)],
        compiler_params=_params("parallel"),
        name="outproj",
    )(hm, ys, w, w, x2d, g_row)


ROUTE_TOKENS = SUBLANES * LANES
ROUTE_CHUNKS = ROUTE_TOKENS // LANES


def _sort_pairs(n):
    pairs = []
    p = 1
    while p < n:
        k = p
        while k >= 1:
            for j in range(k % p, n - k, 2 * k):
                for i in range(min(k, n - j - k)):
                    if (i + j) // (2 * p) == (i + j + k) // (2 * p):
                        pairs.append((i + j, i + j + k))
            k //= 2
        p *= 2
    return pairs


def _first_second(a, b):
    (av, ai), (bv, bi) = a, b
    tie = av == bv
    a_first = jnp.where(tie, bi, av) > jnp.where(tie, ai, bv)
    return ((jnp.maximum(av, bv), jnp.where(a_first, ai, bi)),
            (jnp.minimum(av, bv), jnp.where(a_first, bi, ai)))


def _sorted_group(items):
    items = list(items)
    for i, j in _sort_pairs(len(items)):
        items[i], items[j] = _first_second(items[i], items[j])
    return items


def _merge_top(a, b):
    n = len(a)
    c = [_first_second(a[v], b[n - 1 - v])[0] for v in range(n)]
    j = n // 2
    while j >= 1:
        for i in range(n):
            if (i & j) == 0:
                c[i], c[i + j] = _first_second(c[i], c[i + j])
        j //= 2
    return c


def _top_sorted(groups):
    groups = list(groups)
    while len(groups) > 1:
        groups = [_merge_top(groups[i], groups[i + 1]) for i in range(0, len(groups), 2)]
    return groups[0]


def _product_top(top1, top2):
    K = PEER_TOPK
    pairs = [(r1, r2) for r1 in range(K) for r2 in range(K) if (r1 + 1) * (r2 + 1) <= K]
    cands = [(top1[r1][0] + top2[r2][0], jnp.full_like(top1[0][0], float(r1 * K + r2))) for r1, r2 in pairs]
    pad = (jnp.full_like(top1[0][0], -jnp.inf), jnp.full_like(top1[0][0], float(K * K)))
    cands += [pad] * (-len(cands) % K)
    groups = [cands[:K]] + [_sorted_group(cands[g:g + K]) for g in range(K, len(cands), K)]
    best = _top_sorted(groups)
    top_s = [v for v, _ in best]
    e = [jnp.exp(v - top_s[0]) for v in top_s]
    z = functools.reduce(lambda x, y: x + y, e)
    gates, i1s, i2s = [], [], []
    for r in range(K):
        pos = best[r][1]
        r1 = jnp.floor(pos * np.float32(1.0 / K))
        r2 = pos - r1 * np.float32(K)
        i1, i2 = top1[0][1], top2[0][1]
        for a in range(1, K):
            i1 = jnp.where(r1 == float(a), top1[a][1], i1)
            i2 = jnp.where(r2 == float(a), top2[a][1], i2)
        gates.append(e[r] / z)
        i1s.append(i1)
        i2s.append(i2)
    return i1s, i2s, gates


def _route_kernel(x_ref, wq_ref, sk_ref, i1_ref, i2_ref, gate_ref, q_ref, sc_ref):
    K, n, half = PEER_TOPK, PEER_NKEYS, PEER_QDIM // 2
    h = pl.program_id(1)

    @pl.when(h == 0)
    def _():
        q = jnp.dot(x_ref[...], wq_ref[...], preferred_element_type=F32)
        for hh in range(PEER_HEADS):
            q_ref[hh] = q[:, hh * PEER_QDIM:(hh + 1) * PEER_QDIM].astype(BF16)

    tops = []
    for p in range(2):
        for c in range(ROUTE_CHUNKS):
            qc = q_ref[h, c * LANES:(c + 1) * LANES, p * half:(p + 1) * half]
            sc_ref[pl.ds(c, n, stride=ROUTE_CHUNKS), :] = lax.dot_general(
                sk_ref[p], qc, NT_DIMS, preferred_element_type=F32)
        items = [(sc_ref[k * SUBLANES:(k + 1) * SUBLANES, :], jnp.full((SUBLANES, LANES), float(k), F32))
                 for k in range(n)]
        tops.append(_top_sorted([_sorted_group(items[g:g + K]) for g in range(0, n, K)]))

    i1s, i2s, gates = _product_top(*tops)
    for ref, vals in ((i1_ref, i1s), (i2_ref, i2s), (gate_ref, gates)):
        rows = ref.reshape(ROUTE_CHUNKS * K, LANES)
        for r in range(K):
            rows[pl.ds(r, ROUTE_CHUNKS, stride=K), :] = vals[r]


def _route(xn2, wq, subkeys):
    T, D = xn2.shape
    tt, K = ROUTE_TOKENS, PEER_TOPK
    out = pl.BlockSpec((ROUTE_CHUNKS, K, LANES), lambda i, h: (i, h, 0))
    shape = jax.ShapeDtypeStruct((T // LANES, PEER_HEADS * K, LANES), F32)
    return pl.pallas_call(
        _route_kernel,
        grid=(T // tt, PEER_HEADS),
        in_specs=[
            pl.BlockSpec((tt, D), lambda i, h: (i, 0)),
            pl.BlockSpec(wq.shape, lambda i, h: (0, 0)),
            pl.BlockSpec(subkeys.shape, lambda i, h: (0, 0, 0)),
        ],
        out_specs=[out, out, out],
        out_shape=[shape, shape, shape],
        scratch_shapes=[pltpu.VMEM((PEER_HEADS, tt, PEER_QDIM), BF16),
                        pltpu.VMEM((PEER_NKEYS * ROUTE_CHUNKS, LANES), F32)],
        compiler_params=_params("parallel", "arbitrary"),
        name="peer_route",
    )(xn2, wq, subkeys)


GATE_TOKENS = LANES
GATE_UNROLL = 32


def _gates_kernel(i1_ref, i2_ref, gate_ref, g_ref, a_ref, b_ref, w_ref):
    n = PEER_NKEYS
    a_ref[...] = i1_ref[...].T
    b_ref[...] = i2_ref[...].T
    w_ref[...] = gate_ref[...].T
    keys = lax.broadcasted_iota(jnp.int32, (n, n), 0).astype(F32)

    def body(step, carry):
        base = pl.multiple_of(step * GATE_UNROLL, GATE_UNROLL)
        for u in range(GATE_UNROLL):
            t = base + u
            r1 = a_ref[pl.ds(t, 1), :]
            r2 = b_ref[pl.ds(t, 1), :]
            w = w_ref[pl.ds(t, 1), :]
            at = jnp.where(keys == r1, 1.0, 0.0).astype(BF16)
            bt = jnp.where(keys == r2, w, 0.0).astype(BF16)
            g_ref[t] = lax.dot_general(at, bt, NT_DIMS, preferred_element_type=F32)
        return carry

    lax.fori_loop(0, GATE_TOKENS // GATE_UNROLL, body, 0)


def _gates(i1, i2, gate):
    nchunk, slots, tt = i1.shape
    n = PEER_NKEYS
    spec = pl.BlockSpec((None, slots, tt), lambda i: (i, 0, 0))
    return pl.pallas_call(
        _gates_kernel,
        grid=(nchunk,),
        in_specs=[spec, spec, spec],
        out_specs=pl.BlockSpec((tt, n, n), lambda i: (i, 0, 0)),
        out_shape=jax.ShapeDtypeStruct((nchunk * tt, n, n), F32),
        scratch_shapes=[pltpu.VMEM((tt, slots), F32)] * 3,
        compiler_params=_params("parallel"),
        name="peer_gates",
    )(i1, i2, gate)


EXPERT_G_ROWS = SUBLANES


def _experts_kernel(x_ref, u_ref, v_ref, g_ref, h1_ref, fg_ref, y_ref):
    e, k = pl.program_id(1), pl.program_id(2)
    first = jnp.logical_and(e == 0, k == 0)
    last = jnp.logical_and(e == pl.num_programs(1) - 1, k == pl.num_programs(2) - 1)

    @pl.when(first)
    def _():
        y_ref[...] = jnp.zeros_like(y_ref)

    u = u_ref[...].astype(BF16)
    v = v_ref[...].astype(BF16)
    act = _gelu(lax.dot_general(x_ref[...], u, NT_DIMS, preferred_element_type=F32))
    n = PEER_NKEYS
    tm, rows, _ = g_ref.shape
    nc = u_ref.shape[0] // n
    g_rows = g_ref.reshape(tm * rows, n)
    w = jnp.concatenate([(g_rows[pl.ds(k * nc + c, tm, stride=rows), :] * act[:, c * n:(c + 1) * n]).astype(BF16)
                         for c in range(nc)], axis=1)
    y_ref[...] += jnp.dot(w, v, preferred_element_type=F32)

    @pl.when(last)
    def _():
        h2 = h1_ref[...] + y_ref[...]
        y_ref[...] = h2 * _rms_scale(h2) * fg_ref[...]


def _experts(xn2, u_tab, v_tab, g3, h1, fg_row, tm=1024, te=512):
    T, D = xn2.shape
    E = u_tab.shape[0]
    n = PEER_NKEYS
    sub = EXPERT_G_ROWS * n // te
    once = pl.Buffered(1)
    return pl.pallas_call(
        _experts_kernel,
        grid=(T // tm, E // (te * sub), sub),
        in_specs=[
            pl.BlockSpec((tm, D), lambda i, e, k: (i, 0), pipeline_mode=once),
            pl.BlockSpec((te, D), lambda i, e, k: (e * sub + k, 0)),
            pl.BlockSpec((te, D), lambda i, e, k: (e * sub + k, 0)),
            pl.BlockSpec((tm, EXPERT_G_ROWS, n), lambda i, e, k: (i, e, 0)),
            pl.BlockSpec((tm, D), lambda i, e, k: (i, 0), pipeline_mode=once),
            pl.BlockSpec((1, D), lambda i, e, k: (0, 0)),
        ],
        out_specs=pl.BlockSpec((tm, D), lambda i, e, k: (i, 0), pipeline_mode=once),
        out_shape=jax.ShapeDtypeStruct((T, D), F32),
        compiler_params=_params("parallel", "arbitrary", "arbitrary"),
        name="peer_experts",
    )(xn2, u_tab, v_tab, g3, h1, fg_row)


def kernel(x, norm1_g, w_in, b_gates, conv_qk_w, mlstm_norm_g, s5_lambda_re, s5_lambda_im, s5_log_dt,
           s5_b_re, s5_b_im, s5_c_re, s5_c_im, s5_d, s5_glu_w, w_out, norm2_g, peer_wq, peer_subkeys,
           peer_u, peer_v, final_g):
    B, S, D = x.shape
    W, H = MLSTM_WIDTH, MLSTM_HEADS
    depth = norm1_g.shape[0]
    h = x.reshape(B * S, D)
    for l in range(depth):
        wl = w_in[l].astype(BF16)
        n_gate = 2 * H
        w_u = wl[:, 4 * W + n_gate:]
        w_gates = jnp.pad(wl[:, 4 * W:4 * W + n_gate], ((0, 0), (0, LANES - n_gate)))
        bias_row = jnp.pad(b_gates[l].astype(F32), (0, LANES - n_gate)).reshape(1, LANES)
        s5w = _s5_weights(s5_lambda_re[l].astype(F32), s5_lambda_im[l].astype(F32), s5_log_dt[l],
                          s5_b_re[l].astype(F32), s5_b_im[l].astype(F32), s5_c_re[l].astype(F32),
                          s5_c_im[l].astype(F32), s5_d[l], s5_glu_w[l].astype(F32))
        wo = w_out[l].astype(BF16)

        z, gates = _inproj(h, norm1_g[l].reshape(1, D).astype(F32), wl, w_u, w_gates)
        hm = _mlstm(z, gates, bias_row, conv_qk_w[l].astype(F32),
                    mlstm_norm_g[l].reshape(1, W).astype(F32), B, S)
        ys = _s5(z, *s5w, B, S)
        h1, xn2 = _outproj(hm, ys, wo, h, norm2_g[l].reshape(1, D).astype(F32))

        i1, i2, gate = _route(xn2, peer_wq[l].astype(BF16), peer_subkeys[l].astype(BF16))
        g3 = _gates(i1, i2, gate)
        assert l == depth - 1, "kernel is specialised to a single layer followed by the final norm"
        h = _experts(xn2, peer_u[l], peer_v[l], g3, h1,
                     final_g.reshape(1, D).astype(F32))
    return h.reshape(B, S, D)
```

```python
import functools
import math

import jax
import jax.numpy as jnp
import numpy as np
from jax import lax
from jax.experimental import pallas as pl
from jax.experimental.pallas import tpu as pltpu

F32 = jnp.float32
BF16 = jnp.bfloat16

D_MODEL = 2048
MLSTM_WIDTH = 1024
MLSTM_HEADS = 4
HEAD_DIM = 256
CHUNK = 128
CONV_WIDTH = 4
S5_WIDTH = 1024
S5_GROUP = 16
S5_GROUPS = 64
S5_STATE = 64
PEER_HEADS = 8
PEER_NKEYS = 128
PEER_EXPERTS = PEER_NKEYS * PEER_NKEYS
PEER_TOPK = 16
PEER_QDIM = 256
RMS_EPS = 1e-6

LANES = 128
SUBLANES = 8
VMEM_LIMIT = 56 * 1024 * 1024

NT_DIMS = (((1,), (1,)), ((), ()))


def _params(*sem):
    return pltpu.CompilerParams(dimension_semantics=sem, vmem_limit_bytes=VMEM_LIMIT)


def _rms_scale(v):
    return lax.rsqrt(jnp.mean(v * v, axis=-1, keepdims=True) + RMS_EPS)


def _gelu(v):
    return 0.5 * v * (1.0 + lax.erf(v * np.float32(math.sqrt(0.5))))


def _sigmoid(v):
    return 1.0 / (1.0 + jnp.exp(-v))


def _log_sigmoid(v):
    return jnp.minimum(v, 0.0) - jnp.log1p(jnp.exp(-jnp.abs(v)))


def _inproj_kernel(x_ref, g_ref, w_ref, wu_ref, wg_ref, z_ref, gates_ref, xn_ref):
    j = pl.program_id(1)
    last = pl.num_programs(1) - 1

    @pl.when(j == 0)
    def _():
        x = x_ref[...]
        xn_ref[...] = (x * _rms_scale(x) * g_ref[...]).astype(BF16)
        gates_ref[...] = jnp.dot(xn_ref[...], wg_ref[...], preferred_element_type=F32)

    @pl.when(j < last)
    def _():
        z_ref[...] = jnp.dot(xn_ref[...], w_ref[...], preferred_element_type=F32)

    @pl.when(j == last)
    def _():
        z_ref[...] = jnp.dot(xn_ref[...], wu_ref[...], preferred_element_type=F32)


def _inproj(x2d, g_row, w_all, w_u, w_gates, tm=1024, tn=1024):
    T, D = x2d.shape
    n_main = (4 * MLSTM_WIDTH) // tn
    assert w_u.shape == (D, tn)
    return pl.pallas_call(
        _inproj_kernel,
        grid=(T // tm, n_main + 1),
        in_specs=[
            pl.BlockSpec((tm, D), lambda i, j: (i, 0)),
            pl.BlockSpec((1, D), lambda i, j: (0, 0)),
            pl.BlockSpec((D, tn), lambda i, j: (0, jnp.minimum(j, n_main - 1))),
            pl.BlockSpec((D, tn), lambda i, j: (0, 0)),
            pl.BlockSpec((D, LANES), lambda i, j: (0, 0)),
        ],
        out_specs=[
            pl.BlockSpec((tm, tn), lambda i, j: (i, j)),
            pl.BlockSpec((tm, LANES), lambda i, j: (i, 0)),
        ],
        out_shape=[jax.ShapeDtypeStruct((T, (n_main + 1) * tn), F32), jax.ShapeDtypeStruct((T, LANES), F32)],
        scratch_shapes=[pltpu.VMEM((tm, D), BF16)],
        compiler_params=_params("parallel", "arbitrary"),
        name="inproj",
    )(x2d, g_row, w_all, w_u, w_gates)


def _mlstm_kernel(q_ref, k_ref, v_ref, o_ref, gates_ref, bias_ref, convw_ref, ng_ref, hm_ref,
                  xe_ref, c_ref, n_ref, m_ref):
    L, H, Dh, W = CHUNK, MLSTM_HEADS, HEAD_DIM, MLSTM_WIDTH
    tail = SUBLANES

    @pl.when(pl.program_id(1) == 0)
    def _():
        xe_ref[0:tail, :] = jnp.zeros((tail, 2 * W), F32)
        c_ref[...] = jnp.zeros_like(c_ref)
        n_ref[...] = jnp.zeros_like(n_ref)
        m_ref[...] = jnp.zeros_like(m_ref)

    xe_ref[tail:tail + L, 0:W] = q_ref[...]
    xe_ref[tail:tail + L, W:2 * W] = k_ref[...]
    conv = jnp.zeros((L, 2 * W), F32)
    for j in range(CONV_WIDTH):
        off = tail - (CONV_WIDTH - 1) + j
        conv = conv + convw_ref[j:j + 1, :] * xe_ref[off:off + L, :]
    xe_ref[0:tail, :] = xe_ref[L:L + tail, :]
    qk = conv * _sigmoid(conv)

    g = gates_ref[...] + bias_ref[...]
    g_t = g.T
    row = lax.broadcasted_iota(jnp.int32, (L, L), 0)
    col = lax.broadcasted_iota(jnp.int32, (L, L), 1)
    causal = col <= row
    tri = jnp.where(causal, 1.0, 0.0).astype(F32)
    tri_t = jnp.where(row <= col, 1.0, 0.0).astype(F32)
    hi = lax.Precision.HIGHEST
    a_cols = jnp.dot(tri, _log_sigmoid(g), precision=hi, preferred_element_type=F32)
    a_rows = jnp.dot(_log_sigmoid(g_t), tri_t, precision=hi, preferred_element_type=F32)

    scale = np.float32(Dh ** -0.5)
    for h in range(H):
        a_c = a_cols[:, H + h:H + h + 1]
        a_r = a_rows[H + h:H + h + 1, :]
        li_c = g[:, h:h + 1]
        li_r = g_t[h:h + 1, :]
        a_tot = a_r[:, L - 1:L]
        m0 = m_ref[h]
        c0 = c_ref[h]
        n0 = n_ref[h]

        q = qk[:, h * Dh:(h + 1) * Dh] * scale
        k = qk[:, W + h * Dh:W + (h + 1) * Dh]
        v = v_ref[:, h * Dh:(h + 1) * Dh]
        qb, kb, vb = q.astype(BF16), k.astype(BF16), v.astype(BF16)

        dmat = jnp.where(causal, a_c - a_r + li_r, -jnp.inf)
        m_inter = a_c + m0
        m_t = jnp.maximum(m_inter, jnp.max(dmat, axis=-1, keepdims=True))
        wm = jnp.exp(dmat - m_t)
        s_inter = jnp.exp(m_inter - m_t)
        s = lax.dot_general(qb, kb, NT_DIMS, preferred_element_type=F32) * wm
        num = (jnp.dot(s.astype(BF16), vb, preferred_element_type=F32)
               + s_inter * jnp.dot(qb, c0.astype(BF16), preferred_element_type=F32))
        den = (jnp.sum(s, axis=-1, keepdims=True)
               + s_inter * jnp.sum(q * n0, axis=-1, keepdims=True))
        hh = num / jnp.maximum(jnp.abs(den), jnp.exp(-m_t))
        hh = hh * _rms_scale(hh)
        og = _sigmoid(o_ref[:, h * Dh:(h + 1) * Dh])
        hm_ref[:, h * Dh:(h + 1) * Dh] = (hh * ng_ref[:, h * Dh:(h + 1) * Dh] * og).astype(BF16)

        g_r = a_tot - a_r + li_r
        g_c = a_tot - a_c + li_c
        m_loc = jnp.max(g_r, axis=-1, keepdims=True)
        kw = k * jnp.exp(g_c - m_loc)
        d_c = jnp.dot(kw.T.astype(BF16), vb, preferred_element_type=F32)
        d_n = jnp.sum(kw, axis=0, keepdims=True)
        m_new = jnp.maximum(a_tot + m0, m_loc)
        s_old = jnp.exp(a_tot + m0 - m_new)
        s_new = jnp.exp(m_loc - m_new)
        c_ref[h] = s_old * c0 + s_new * d_c
        n_ref[h] = s_old * n0 + s_new * d_n
        m_ref[h] = m_new


def _mlstm(z, gates, bias_row, conv_w, norm_g_row, batch, seq):
    T = z.shape[0]
    L, W = CHUNK, MLSTM_WIDTH
    nc = seq // L
    blk = lambda col: pl.BlockSpec((L, W), lambda b, c, col=col: (b * nc + c, col))
    return pl.pallas_call(
        _mlstm_kernel,
        grid=(batch, nc),
        in_specs=[
            blk(0), blk(1), blk(2), blk(3),
            pl.BlockSpec((L, LANES), lambda b, c: (b * nc + c, 0)),
            pl.BlockSpec((1, LANES), lambda b, c: (0, 0)),
            pl.BlockSpec((CONV_WIDTH, 2 * W), lambda b, c: (0, 0)),
            pl.BlockSpec((1, W), lambda b, c: (0, 0)),
        ],
        out_specs=pl.BlockSpec((L, W), lambda b, c: (b * nc + c, 0)),
        out_shape=jax.ShapeDtypeStruct((T, W), BF16),
        scratch_shapes=[
            pltpu.VMEM((L + 2 * SUBLANES, 2 * W), F32),
            pltpu.VMEM((MLSTM_HEADS, HEAD_DIM, HEAD_DIM), F32),
            pltpu.VMEM((MLSTM_HEADS, 1, HEAD_DIM), F32),
            pltpu.VMEM((MLSTM_HEADS, 1, 1), F32),
        ],
        compiler_params=_params("parallel", "arbitrary"),
        name="mlstm",
    )(z, z, z, z, gates, bias_row, conv_w, norm_g_row)


S5_BLOCK = 256
S5_SEG = S5_BLOCK // SUBLANES
S5_TILES = S5_WIDTH // LANES
S5_TILE_STATE = 8 * S5_STATE


def _s5_kernel(u_ref, win_ref, wout_ref, wglu_ref, a_ref, aseg_ref, d_ref, ys_ref,
               up_ref, x_ref, xs_ref, yp_ref, carry_ref):
    nseg, seg, half = SUBLANES, S5_SEG, S5_TILE_STATE
    chains = range(u_ref.shape[0])

    @pl.when(pl.program_id(1) == 0)
    def _():
        carry_ref[...] = jnp.zeros_like(carry_ref)

    def cmul_add(ar, ai, sr, si, xr, xi):
        return ar * sr - ai * si + xr, ar * si + ai * sr + xi

    def x_rows(b, i):
        rows = slice(nseg * i, nseg * (i + 1))
        return x_ref[b, rows, 0:half], x_ref[b, rows, half:2 * half]

    for i in range(seg):
        for b in chains:
            up_ref[b, nseg * i:nseg * (i + 1), :] = u_ref[b, pl.ds(i, nseg, stride=seg), :]
    for b in chains:
        x_ref[b] = jnp.dot(up_ref[b].astype(BF16), win_ref[...], preferred_element_type=F32)
    ar = a_ref[:, 0:half]
    ai = a_ref[:, half:2 * half]

    fin = [(jnp.zeros((nseg, half), F32), jnp.zeros((nseg, half), F32)) for _ in chains]
    for i in range(seg):
        for b in chains:
            fin[b] = cmul_add(ar, ai, *fin[b], *x_rows(b, i))

    lr = aseg_ref[:, 0:half]
    lim = aseg_ref[:, half:2 * half]
    state = []
    for b in chains:
        cr = carry_ref[b, :, 0:half]
        ci = carry_ref[b, :, half:2 * half]
        crs, cis = [], []
        for r in range(nseg):
            crs.append(cr)
            cis.append(ci)
            cr, ci = cmul_add(lr, lim, cr, ci, fin[b][0][r:r + 1, :], fin[b][1][r:r + 1, :])
        carry_ref[b, :, 0:half] = cr
        carry_ref[b, :, half:2 * half] = ci
        state.append((jnp.concatenate(crs, axis=0), jnp.concatenate(cis, axis=0)))

    for i in range(seg):
        rows = slice(nseg * i, nseg * (i + 1))
        for b in chains:
            state[b] = cmul_add(ar, ai, *state[b], *x_rows(b, i))
            xs_ref[b, rows, 0:half] = state[b][0]
            xs_ref[b, rows, half:2 * half] = state[b][1]

    for b in chains:
        y = jnp.dot(xs_ref[b].astype(BF16), wout_ref[...], preferred_element_type=F32)
        y = y + d_ref[...] * up_ref[b]
        ab = jnp.dot(_gelu(y).astype(BF16), wglu_ref[...], preferred_element_type=F32)
        yp_ref[b] = ab[:, 0:LANES] * _sigmoid(ab[:, LANES:2 * LANES])
    for i in range(seg):
        for b in chains:
            ys_ref[b, pl.ds(i, nseg, stride=seg), :] = yp_ref[b, nseg * i:nseg * (i + 1), :]


def _s5(z, win, wout, wglu, a_rows, aseg_rows, d_row, batch, seq):
    T, N = z.shape
    blk = S5_BLOCK
    nb = seq // blk
    u_col = (4 * MLSTM_WIDTH) // LANES
    tile = lambda shape: pl.BlockSpec((None,) + shape, lambda j, c: (j,) + (0,) * len(shape))
    ys = pl.pallas_call(
        _s5_kernel,
        grid=(S5_TILES, nb),
        in_specs=[
            pl.BlockSpec((batch, blk, LANES), lambda j, c: (0, c, u_col + j)),
            tile(win.shape[1:]), tile(wout.shape[1:]), tile(wglu.shape[1:]),
            tile(a_rows.shape[1:]), tile(aseg_rows.shape[1:]),
            pl.BlockSpec((1, LANES), lambda j, c: (0, j)),
        ],
        out_specs=pl.BlockSpec((batch, blk, LANES), lambda j, c: (0, c, j)),
        out_shape=jax.ShapeDtypeStruct((batch, seq, S5_WIDTH), F32),
        scratch_shapes=[
            pltpu.VMEM((batch, blk, LANES), F32),
            pltpu.VMEM((batch, blk, 2 * S5_TILE_STATE), F32),
            pltpu.VMEM((batch, blk, 2 * S5_TILE_STATE), F32),
            pltpu.VMEM((batch, blk, LANES), F32),
            pltpu.VMEM((batch, 1, 2 * S5_TILE_STATE), F32),
        ],
        compiler_params=_params("parallel", "arbitrary"),
        name="s5",
    )(z.reshape(batch, seq, N), win, wout, wglu, a_rows, aseg_rows, d_row)
    return ys.reshape(T, S5_WIDTH)


def _s5_weights(lam_re, lam_im, log_dt, b_re, b_im, c_re, c_im, d, glu_w):
    G, P, Hg = S5_GROUPS, S5_STATE, S5_GROUP
    nt, gl = S5_TILES, G // S5_TILES
    dt = jnp.exp(log_dt.astype(F32))[:, None]
    mag = jnp.exp(lam_re * dt)
    ar = mag * jnp.cos(lam_im * dt)
    ai = mag * jnp.sin(lam_im * dt)
    den = lam_re * lam_re + lam_im * lam_im
    fr = ((ar - 1.0) * lam_re + ai * lam_im) / den
    fi = (ai * lam_re - (ar - 1.0) * lam_im) / den
    bbr = fr[..., None] * b_re - fi[..., None] * b_im
    bbi = fr[..., None] * b_im + fi[..., None] * b_re
    eye = jnp.eye(gl, dtype=F32)

    def tiles(v):
        return v.reshape((nt, gl) + v.shape[1:])

    bb = jnp.stack([tiles(bbr), tiles(bbi)], axis=2)
    win = jnp.einsum('tgrph,gk->tghrkp', bb, eye).reshape(nt, gl * Hg, 2 * gl * P)
    cc = jnp.stack([tiles(c_re), -tiles(c_im)], axis=2)
    wout = jnp.einsum('tgrhp,gk->trkpgh', cc, eye).reshape(nt, 2 * gl * P, gl * Hg)
    gw = tiles(glu_w).reshape(nt, gl, Hg, 2, Hg)
    wglu = jnp.einsum('tghrk,gq->tqhrgk', gw, eye).reshape(nt, gl * Hg, 2 * gl * Hg)

    def rows(re, im):
        return jnp.concatenate([tiles(re).reshape(nt, 1, gl * P), tiles(im).reshape(nt, 1, gl * P)], axis=-1)

    a_rows = jnp.broadcast_to(rows(ar, ai), (nt, SUBLANES, 2 * gl * P))
    pr, pi = ar, ai
    for _ in range(int(math.log2(S5_SEG))):
        pr, pi = pr * pr - pi * pi, 2.0 * pr * pi
    aseg_rows = rows(pr, pi)
    d_row = d.reshape(1, G * Hg).astype(F32)
    return win.astype(BF16), wout.astype(BF16), wglu.astype(BF16), a_rows, aseg_rows, d_row


def _outproj_kernel(hm_ref, ys_ref, w1_ref, w2_ref, x_ref, g_ref, h1_ref, xn_ref):
    mix = (jnp.dot(hm_ref[...], w1_ref[...], preferred_element_type=F32)
           + jnp.dot(ys_ref[...].astype(BF16), w2_ref[...], preferred_element_type=F32))
    h1 = x_ref[...] + mix
    h1_ref[...] = h1
    xn_ref[...] = (h1 * _rms_scale(h1) * g_ref[...]).astype(BF16)


def _outproj(hm, ys, w, x2d, g_row, tm=512):
    T, D = x2d.shape
    W = hm.shape[1]
    row = lambda w: pl.BlockSpec((tm, w), lambda i: (i, 0))
    half = lambda k: pl.BlockSpec((W, D), lambda i: (k, 0))
    return pl.pallas_call(
        _outproj_kernel,
        grid=(T // tm,),
        in_specs=[row(W), row(W), half(0), half(1), row(D), pl.BlockSpec((1, D), lambda i: (0, 0))],
        out_specs=[row(D), row(D)],
        out_shape=[jax.ShapeDtypeStruct((T, D), F32), jax.ShapeDtypeStruct((T, D), BF16)],
        compiler_params=_params("parallel"),
        name="outproj",
    )(hm, ys, w, w, x2d, g_row)


ROUTE_GROUP = 1024
ROUTE_CHUNK = LANES
GATE_PIECE = 4
GATE_LAG = 1
assert PEER_HEADS == SUBLANES


def _sort_pairs(n):
    pairs = []
    p = 1
    while p < n:
        k = p
        while k >= 1:
            for j in range(k % p, n - k, 2 * k):
                for i in range(min(k, n - j - k)):
                    if (i + j) // (2 * p) == (i + j + k) // (2 * p):
                        pairs.append((i + j, i + j + k))
            k //= 2
        p *= 2
    return pairs


def _first_second(a, b):
    (av, ai), (bv, bi) = a, b
    tie = av == bv
    a_first = jnp.where(tie, bi, av) > jnp.where(tie, ai, bv)
    return ((jnp.maximum(av, bv), jnp.where(a_first, ai, bi)),
            (jnp.minimum(av, bv), jnp.where(a_first, bi, ai)))


def _sorted_group(items):
    items = list(items)
    for i, j in _sort_pairs(len(items)):
        items[i], items[j] = _first_second(items[i], items[j])
    return items


def _merge_top(a, b):
    n = len(a)
    c = [_first_second(a[v], b[n - 1 - v])[0] for v in range(n)]
    j = n // 2
    while j >= 1:
        for i in range(n):
            if (i & j) == 0:
                c[i], c[i + j] = _first_second(c[i], c[i + j])
        j //= 2
    return c


def _top_sorted(groups, stage=lambda items: items):
    groups = list(groups)
    while len(groups) > 1:
        groups = [_merge_top(stage(groups[i]), groups[i + 1]) for i in range(0, len(groups), 2)]
    return groups[0]


def _product_top(top1, top2, stage=lambda items: items):
    K = PEER_TOPK
    pairs = [(r1, r2) for r1 in range(K) for r2 in range(K) if (r1 + 1) * (r2 + 1) <= K]
    cands = [(top1[r1][0] + top2[r2][0], jnp.full_like(top1[0][0], float(r1 * K + r2))) for r1, r2 in pairs]
    pad = (jnp.full_like(top1[0][0], -jnp.inf), jnp.full_like(top1[0][0], float(K * K)))
    cands += [pad] * (-len(cands) % K)
    groups = [cands[:K]]
    for g in range(K, len(cands), K):
        groups.append(_sorted_group(stage(cands[g:g + K])))
    best = _top_sorted(groups, stage)
    top_s = [v for v, _ in best]
    e = [jnp.exp(v - top_s[0]) for v in top_s]
    z = functools.reduce(lambda x, y: x + y, e)
    gates, i1s, i2s = [], [], []
    for r in range(K):
        pos = best[r][1]
        r1 = jnp.floor(pos * np.float32(1.0 / K))
        r2 = pos - r1 * np.float32(K)
        i1, i2 = top1[0][1], top2[0][1]
        for a in range(1, K):
            i1 = jnp.where(r1 == float(a), top1[a][1], i1)
            i2 = jnp.where(r2 == float(a), top2[a][1], i2)
        gates.append(e[r] / z)
        i1s.append(i1)
        i2s.append(i2)
    return i1s, i2s, gates


def _route_gates_kernel(x_ref, wq_ref, sk_ref, g_ref, q_ref, sc_ref, slot_ref, tok_ref):
    K, n, half, H = PEER_TOPK, PEER_NKEYS, PEER_QDIM // 2, PEER_HEADS
    s = pl.program_id(0)
    n_chunks = pl.num_programs(0) - 1
    per_group = ROUTE_GROUP // ROUTE_CHUNK
    cur = s % 2

    @pl.when(s == 0)
    def _():
        slot_ref[...] = jnp.zeros_like(slot_ref)

    @pl.when(jnp.logical_and(s % per_group == 0, s < n_chunks))
    def _():
        q = jnp.dot(x_ref[...], wq_ref[...], preferred_element_type=F32)
        for hh in range(H):
            q_ref[hh] = q[:, hh * PEER_QDIM:(hh + 1) * PEER_QDIM].astype(BF16)

    for a in range(3):
        tok_ref[a] = slot_ref[1 - cur, a].T
    keys = lax.broadcasted_iota(jnp.int32, (n, n), 0).astype(F32)
    pieces = iter(range(ROUTE_CHUNK // GATE_PIECE))
    done = []

    def gate_piece():
        k = next(pieces, None)
        if k is None:
            return
        for t in range(k * GATE_PIECE, (k + 1) * GATE_PIECE):
            at = jnp.where(keys == tok_ref[0, t:t + 1, :], 1.0, 0.0).astype(BF16)
            bt = jnp.where(keys == tok_ref[1, t:t + 1, :], tok_ref[2, t:t + 1, :], 0.0).astype(BF16)
            g = lax.dot_general(at, bt, NT_DIMS, preferred_element_type=F32)
            g_ref[t] = g
        done.append(g[0:SUBLANES, :] * 0.0)

    def stage(items):
        gate_piece()
        if len(done) > GATE_LAG:
            (v, i), rest = items[0], list(items[1:])
            return [(v + done[len(done) - 1 - GATE_LAG], i)] + rest
        return items

    row0 = pl.multiple_of((s % per_group) * ROUTE_CHUNK, ROUTE_CHUNK)
    tops = []
    for p in range(2):
        for h in range(H):
            qc = q_ref[h, pl.ds(row0, ROUTE_CHUNK), p * half:(p + 1) * half]
            sc_ref[p, pl.ds(h, n, stride=H), :] = lax.dot_general(
                sk_ref[p], qc, NT_DIMS, preferred_element_type=F32)
        items = [(sc_ref[p, k * SUBLANES:(k + 1) * SUBLANES, :], jnp.full((SUBLANES, LANES), float(k), F32))
                 for k in range(n)]
        groups = [_sorted_group(stage(items[g:g + K])) for g in range(0, n, K)]
        tops.append(_top_sorted(groups, stage))
    i1s, i2s, gates = _product_top(*tops, stage)
    for _ in range(ROUTE_CHUNK // GATE_PIECE):
        gate_piece()
    for a, vals in enumerate((i1s, i2s, gates)):
        for r in range(K):
            slot_ref[cur, a, pl.ds(r, H, stride=K), :] = vals[r]


def _route_gates(xn2, wq, subkeys):
    T, D = xn2.shape
    n, slots = PEER_NKEYS, PEER_HEADS * PEER_TOPK
    n_chunks = T // ROUTE_CHUNK
    per_group = ROUTE_GROUP // ROUTE_CHUNK
    n_groups = T // ROUTE_GROUP
    once = pl.Buffered(1)
    return pl.pallas_call(
        _route_gates_kernel,
        grid=(n_chunks + 1,),
        in_specs=[
            pl.BlockSpec((ROUTE_GROUP, D), lambda s: (jnp.minimum(s // per_group, n_groups - 1), 0)),
            pl.BlockSpec(wq.shape, lambda s: (0, 0), pipeline_mode=once),
            pl.BlockSpec(subkeys.shape, lambda s: (0, 0, 0)),
        ],
        out_specs=pl.BlockSpec((ROUTE_CHUNK, n, n), lambda s: (jnp.maximum(s - 1, 0), 0, 0)),
        out_shape=jax.ShapeDtypeStruct((T, n, n), F32),
        scratch_shapes=[pltpu.VMEM((PEER_HEADS, ROUTE_GROUP, PEER_QDIM), BF16),
                        pltpu.VMEM((2, n * SUBLANES, LANES), F32),
                        pltpu.VMEM((2, 3, slots, ROUTE_CHUNK), F32),
                        pltpu.VMEM((3, ROUTE_CHUNK, slots), F32)],
        compiler_params=_params("arbitrary"),
        name="peer_route_gates",
    )(xn2, wq, subkeys)


EXPERT_G_ROWS = SUBLANES


def _experts_kernel(x_ref, u_ref, v_ref, g_ref, h1_ref, fg_ref, y_ref):
    e, k = pl.program_id(1), pl.program_id(2)
    first = jnp.logical_and(e == 0, k == 0)
    last = jnp.logical_and(e == pl.num_programs(1) - 1, k == pl.num_programs(2) - 1)

    @pl.when(first)
    def _():
        y_ref[...] = jnp.zeros_like(y_ref)

    u = u_ref[...].astype(BF16)
    v = v_ref[...].astype(BF16)
    act = _gelu(lax.dot_general(x_ref[...], u, NT_DIMS, preferred_element_type=F32))
    n = PEER_NKEYS
    tm, rows, _ = g_ref.shape
    nc = u_ref.shape[0] // n
    g_rows = g_ref.reshape(tm * rows, n)
    w = jnp.concatenate([(g_rows[pl.ds(k * nc + c, tm, stride=rows), :] * act[:, c * n:(c + 1) * n]).astype(BF16)
                         for c in range(nc)], axis=1)
    y_ref[...] += jnp.dot(w, v, preferred_element_type=F32)

    @pl.when(last)
    def _():
        h2 = h1_ref[...] + y_ref[...]
        y_ref[...] = h2 * _rms_scale(h2) * fg_ref[...]


def _experts(xn2, u_tab, v_tab, g3, h1, fg_row, tm=1024, te=512):
    T, D = xn2.shape
    E = u_tab.shape[0]
    n = PEER_NKEYS
    sub = EXPERT_G_ROWS * n // te
    once = pl.Buffered(1)
    return pl.pallas_call(
        _experts_kernel,
        grid=(T // tm, E // (te * sub), sub),
        in_specs=[
            pl.BlockSpec((tm, D), lambda i, e, k: (i, 0), pipeline_mode=once),
            pl.BlockSpec((te, D), lambda i, e, k: (e * sub + k, 0)),
            pl.BlockSpec((te, D), lambda i, e, k: (e * sub + k, 0)),
            pl.BlockSpec((tm, EXPERT_G_ROWS, n), lambda i, e, k: (i, e, 0)),
            pl.BlockSpec((tm, D), lambda i, e, k: (i, 0), pipeline_mode=once),
            pl.BlockSpec((1, D), lambda i, e, k: (0, 0)),
        ],
        out_specs=pl.BlockSpec((tm, D), lambda i, e, k: (i, 0), pipeline_mode=once),
        out_shape=jax.ShapeDtypeStruct((T, D), F32),
        compiler_params=_params("parallel", "arbitrary", "arbitrary"),
        name="peer_experts",
    )(xn2, u_tab, v_tab, g3, h1, fg_row)


def kernel(x, norm1_g, w_in, b_gates, conv_qk_w, mlstm_norm_g, s5_lambda_re, s5_lambda_im, s5_log_dt,
           s5_b_re, s5_b_im, s5_c_re, s5_c_im, s5_d, s5_glu_w, w_out, norm2_g, peer_wq, peer_subkeys,
           peer_u, peer_v, final_g):
    B, S, D = x.shape
    W, H = MLSTM_WIDTH, MLSTM_HEADS
    depth = norm1_g.shape[0]
    h = x.reshape(B * S, D)
    for l in range(depth):
        wl = w_in[l].astype(BF16)
        n_gate = 2 * H
        w_u = wl[:, 4 * W + n_gate:]
        w_gates = jnp.pad(wl[:, 4 * W:4 * W + n_gate], ((0, 0), (0, LANES - n_gate)))
        bias_row = jnp.pad(b_gates[l].astype(F32), (0, LANES - n_gate)).reshape(1, LANES)
        s5w = _s5_weights(s5_lambda_re[l].astype(F32), s5_lambda_im[l].astype(F32), s5_log_dt[l],
                          s5_b_re[l].astype(F32), s5_b_im[l].astype(F32), s5_c_re[l].astype(F32),
                          s5_c_im[l].astype(F32), s5_d[l], s5_glu_w[l].astype(F32))
        wo = w_out[l].astype(BF16)

        z, gates = _inproj(h, norm1_g[l].reshape(1, D).astype(F32), wl, w_u, w_gates)
        hm = _mlstm(z, gates, bias_row, conv_qk_w[l].astype(F32),
                    mlstm_norm_g[l].reshape(1, W).astype(F32), B, S)
        ys = _s5(z, *s5w, B, S)
        h1, xn2 = _outproj(hm, ys, wo, h, norm2_g[l].reshape(1, D).astype(F32))

        g3 = _route_gates(xn2, peer_wq[l].astype(BF16), peer_subkeys[l].astype(BF16))
        assert l == depth - 1, "kernel is specialised to a single layer followed by the final norm"
        h = _experts(xn2, peer_u[l], peer_v[l], g3, h1,
                     final_g.reshape(1, D).astype(F32))
    return h.reshape(B, S, D)
```

```python
import functools
import math

import jax
import jax.numpy as jnp
import numpy as np
from jax import lax
from jax.experimental import pallas as pl
from jax.experimental.pallas import tpu as pltpu

F32 = jnp.float32
BF16 = jnp.bfloat16

D_MODEL = 2048
MLSTM_WIDTH = 1024
MLSTM_HEADS = 4
HEAD_DIM = 256
CHUNK = 128
CONV_WIDTH = 4
S5_WIDTH = 1024
S5_GROUP = 16
S5_GROUPS = 64
S5_STATE = 64
PEER_HEADS = 8
PEER_NKEYS = 128
PEER_EXPERTS = PEER_NKEYS * PEER_NKEYS
PEER_TOPK = 16
PEER_QDIM = 256
RMS_EPS = 1e-6

LANES = 128
SUBLANES = 8
VMEM_LIMIT = 56 * 1024 * 1024

NT_DIMS = (((1,), (1,)), ((), ()))


def _params(*sem):
    return pltpu.CompilerParams(dimension_semantics=sem, vmem_limit_bytes=VMEM_LIMIT)


def _rms_scale(v):
    return lax.rsqrt(jnp.mean(v * v, axis=-1, keepdims=True) + RMS_EPS)


def _gelu(v):
    return 0.5 * v * (1.0 + lax.erf(v * np.float32(math.sqrt(0.5))))


def _sigmoid(v):
    return 1.0 / (1.0 + jnp.exp(-v))


def _log_sigmoid(v):
    return jnp.minimum(v, 0.0) - jnp.log1p(jnp.exp(-jnp.abs(v)))


def _inproj_kernel(x_ref, g_ref, w_ref, wu_ref, wg_ref, z_ref, gates_ref, xn_ref):
    j = pl.program_id(1)
    last = pl.num_programs(1) - 1

    @pl.when(j == 0)
    def _():
        x = x_ref[...]
        xn_ref[...] = (x * _rms_scale(x) * g_ref[...]).astype(BF16)
        gates_ref[...] = jnp.dot(xn_ref[...], wg_ref[...], preferred_element_type=F32)

    @pl.when(j < last)
    def _():
        z_ref[...] = jnp.dot(xn_ref[...], w_ref[...], preferred_element_type=F32)

    @pl.when(j == last)
    def _():
        z_ref[...] = jnp.dot(xn_ref[...], wu_ref[...], preferred_element_type=F32)


def _inproj(x2d, g_row, w_all, w_u, w_gates, tm=1024, tn=1024):
    T, D = x2d.shape
    n_main = (4 * MLSTM_WIDTH) // tn
    assert w_u.shape == (D, tn)
    return pl.pallas_call(
        _inproj_kernel,
        grid=(T // tm, n_main + 1),
        in_specs=[
            pl.BlockSpec((tm, D), lambda i, j: (i, 0)),
            pl.BlockSpec((1, D), lambda i, j: (0, 0)),
            pl.BlockSpec((D, tn), lambda i, j: (0, jnp.minimum(j, n_main - 1))),
            pl.BlockSpec((D, tn), lambda i, j: (0, 0)),
            pl.BlockSpec((D, LANES), lambda i, j: (0, 0)),
        ],
        out_specs=[
            pl.BlockSpec((tm, tn), lambda i, j: (i, j)),
            pl.BlockSpec((tm, LANES), lambda i, j: (i, 0)),
        ],
        out_shape=[jax.ShapeDtypeStruct((T, (n_main + 1) * tn), F32), jax.ShapeDtypeStruct((T, LANES), F32)],
        scratch_shapes=[pltpu.VMEM((tm, D), BF16)],
        compiler_params=_params("parallel", "arbitrary"),
        name="inproj",
    )(x2d, g_row, w_all, w_u, w_gates)


def _mlstm_kernel(q_ref, k_ref, v_ref, o_ref, gates_ref, bias_ref, convw_ref, ng_ref, hm_ref,
                  xe_ref, c_ref, n_ref, m_ref):
    L, H, Dh, W = CHUNK, MLSTM_HEADS, HEAD_DIM, MLSTM_WIDTH
    tail = SUBLANES

    @pl.when(pl.program_id(1) == 0)
    def _():
        xe_ref[0:tail, :] = jnp.zeros((tail, 2 * W), F32)
        c_ref[...] = jnp.zeros_like(c_ref)
        n_ref[...] = jnp.zeros_like(n_ref)
        m_ref[...] = jnp.zeros_like(m_ref)

    xe_ref[tail:tail + L, 0:W] = q_ref[...]
    xe_ref[tail:tail + L, W:2 * W] = k_ref[...]
    ext = xe_ref[0:tail + L, :].astype(BF16)
    pick_r = lax.broadcasted_iota(jnp.int32, ((CONV_WIDTH - 1) * L, tail + L), 0)
    pick_c = lax.broadcasted_iota(jnp.int32, ((CONV_WIDTH - 1) * L, tail + L), 1)
    pick = jnp.where(pick_c == (pick_r % L) + (tail - (CONV_WIDTH - 1)) + pick_r // L, 1.0, 0.0).astype(BF16)
    shifted = jnp.dot(pick, ext, preferred_element_type=F32)
    conv = convw_ref[CONV_WIDTH - 1:CONV_WIDTH, :] * xe_ref[tail:tail + L, :]
    for j in range(CONV_WIDTH - 1):
        conv = conv + convw_ref[j:j + 1, :] * shifted[j * L:(j + 1) * L, :]
    xe_ref[0:tail, :] = xe_ref[L:L + tail, :]
    qk = conv * _sigmoid(conv)

    g = gates_ref[...] + bias_ref[...]
    g_t = g.T
    row = lax.broadcasted_iota(jnp.int32, (L, L), 0)
    col = lax.broadcasted_iota(jnp.int32, (L, L), 1)
    causal = col <= row
    tri = jnp.where(causal, 1.0, 0.0).astype(F32)
    tri_t = jnp.where(row <= col, 1.0, 0.0).astype(F32)
    hi = lax.Precision.HIGHEST
    a_cols = jnp.dot(tri, _log_sigmoid(g), precision=hi, preferred_element_type=F32)
    a_rows = jnp.dot(_log_sigmoid(g_t), tri_t, precision=hi, preferred_element_type=F32)

    scale = np.float32(Dh ** -0.5)
    for h in range(H):
        a_c = a_cols[:, H + h:H + h + 1]
        a_r = a_rows[H + h:H + h + 1, :]
        li_c = g[:, h:h + 1]
        li_r = g_t[h:h + 1, :]
        a_tot = a_r[:, L - 1:L]
        m0 = m_ref[h]
        c0 = c_ref[h]
        n0 = n_ref[h]

        q = qk[:, h * Dh:(h + 1) * Dh] * scale
        k = qk[:, W + h * Dh:W + (h + 1) * Dh]
        v = v_ref[:, h * Dh:(h + 1) * Dh]
        qb, kb, vb = q.astype(BF16), k.astype(BF16), v.astype(BF16)

        dmat = jnp.where(causal, a_c - a_r + li_r, -jnp.inf)
        m_inter = a_c + m0
        m_t = jnp.maximum(m_inter, jnp.max(dmat, axis=-1, keepdims=True))
        wm = jnp.exp(dmat - m_t)
        s_inter = jnp.exp(m_inter - m_t)
        s = lax.dot_general(qb, kb, NT_DIMS, preferred_element_type=F32) * wm
        num = (jnp.dot(s.astype(BF16), vb, preferred_element_type=F32)
               + s_inter * jnp.dot(qb, c0.astype(BF16), preferred_element_type=F32))
        den = (jnp.sum(s, axis=-1, keepdims=True)
               + s_inter * jnp.sum(q * n0, axis=-1, keepdims=True))
        hh = num / jnp.maximum(jnp.abs(den), jnp.exp(-m_t))
        hh = hh * _rms_scale(hh)
        og = _sigmoid(o_ref[:, h * Dh:(h + 1) * Dh])
        hm_ref[:, h * Dh:(h + 1) * Dh] = (hh * ng_ref[:, h * Dh:(h + 1) * Dh] * og).astype(BF16)

        g_r = a_tot - a_r + li_r
        g_c = a_tot - a_c + li_c
        m_loc = jnp.max(g_r, axis=-1, keepdims=True)
        kw = k * jnp.exp(g_c - m_loc)
        d_c = jnp.dot(kw.T.astype(BF16), vb, preferred_element_type=F32)
        d_n = jnp.sum(kw, axis=0, keepdims=True)
        m_new = jnp.maximum(a_tot + m0, m_loc)
        s_old = jnp.exp(a_tot + m0 - m_new)
        s_new = jnp.exp(m_loc - m_new)
        c_ref[h] = s_old * c0 + s_new * d_c
        n_ref[h] = s_old * n0 + s_new * d_n
        m_ref[h] = m_new


def _mlstm(z, gates, bias_row, conv_w, norm_g_row, batch, seq):
    T = z.shape[0]
    L, W = CHUNK, MLSTM_WIDTH
    nc = seq // L
    blk = lambda col: pl.BlockSpec((L, W), lambda b, c, col=col: (b * nc + c, col))
    return pl.pallas_call(
        _mlstm_kernel,
        grid=(batch, nc),
        in_specs=[
            blk(0), blk(1), blk(2), blk(3),
            pl.BlockSpec((L, LANES), lambda b, c: (b * nc + c, 0)),
            pl.BlockSpec((1, LANES), lambda b, c: (0, 0)),
            pl.BlockSpec((CONV_WIDTH, 2 * W), lambda b, c: (0, 0)),
            pl.BlockSpec((1, W), lambda b, c: (0, 0)),
        ],
        out_specs=pl.BlockSpec((L, W), lambda b, c: (b * nc + c, 0)),
        out_shape=jax.ShapeDtypeStruct((T, W), BF16),
        scratch_shapes=[
            pltpu.VMEM((L + 2 * SUBLANES, 2 * W), F32),
            pltpu.VMEM((MLSTM_HEADS, HEAD_DIM, HEAD_DIM), F32),
            pltpu.VMEM((MLSTM_HEADS, 1, HEAD_DIM), F32),
            pltpu.VMEM((MLSTM_HEADS, 1, 1), F32),
        ],
        compiler_params=_params("parallel", "arbitrary"),
        name="mlstm",
    )(z, z, z, z, gates, bias_row, conv_w, norm_g_row)


S5_BLOCK = 256
S5_SEG = S5_BLOCK // SUBLANES
S5_TILES = S5_WIDTH // LANES
S5_TILE_STATE = 8 * S5_STATE


def _s5_kernel(u_ref, win_ref, wout_ref, wglu_ref, a_ref, aseg_ref, d_ref, ys_ref,
               up_ref, x_ref, xs_ref, yp_ref, carry_ref):
    nseg, seg, half = SUBLANES, S5_SEG, S5_TILE_STATE
    chains = range(u_ref.shape[0])

    @pl.when(pl.program_id(1) == 0)
    def _():
        carry_ref[...] = jnp.zeros_like(carry_ref)

    def cmul_add(ar, ai, sr, si, xr, xi):
        return ar * sr - ai * si + xr, ar * si + ai * sr + xi

    def x_rows(b, i):
        rows = slice(nseg * i, nseg * (i + 1))
        return x_ref[b, rows, 0:half], x_ref[b, rows, half:2 * half]

    for i in range(seg):
        for b in chains:
            up_ref[b, nseg * i:nseg * (i + 1), :] = u_ref[b, pl.ds(i, nseg, stride=seg), :]
    for b in chains:
        x_ref[b] = jnp.dot(up_ref[b].astype(BF16), win_ref[...], preferred_element_type=F32)
    ar = a_ref[:, 0:half]
    ai = a_ref[:, half:2 * half]

    fin = [(jnp.zeros((nseg, half), F32), jnp.zeros((nseg, half), F32)) for _ in chains]
    for i in range(seg):
        for b in chains:
            fin[b] = cmul_add(ar, ai, *fin[b], *x_rows(b, i))

    lr = aseg_ref[:, 0:half]
    lim = aseg_ref[:, half:2 * half]
    state = []
    for b in chains:
        cr = carry_ref[b, :, 0:half]
        ci = carry_ref[b, :, half:2 * half]
        crs, cis = [], []
        for r in range(nseg):
            crs.append(cr)
            cis.append(ci)
            cr, ci = cmul_add(lr, lim, cr, ci, fin[b][0][r:r + 1, :], fin[b][1][r:r + 1, :])
        carry_ref[b, :, 0:half] = cr
        carry_ref[b, :, half:2 * half] = ci
        state.append((jnp.concatenate(crs, axis=0), jnp.concatenate(cis, axis=0)))

    for i in range(seg):
        rows = slice(nseg * i, nseg * (i + 1))
        for b in chains:
            state[b] = cmul_add(ar, ai, *state[b], *x_rows(b, i))
            xs_ref[b, rows, 0:half] = state[b][0]
            xs_ref[b, rows, half:2 * half] = state[b][1]

    for b in chains:
        y = jnp.dot(xs_ref[b].astype(BF16), wout_ref[...], preferred_element_type=F32)
        y = y + d_ref[...] * up_ref[b]
        ab = jnp.dot(_gelu(y).astype(BF16), wglu_ref[...], preferred_element_type=F32)
        yp_ref[b] = ab[:, 0:LANES] * _sigmoid(ab[:, LANES:2 * LANES])
    for i in range(seg):
        for b in chains:
            ys_ref[b, pl.ds(i, nseg, stride=seg), :] = yp_ref[b, nseg * i:nseg * (i + 1), :]


def _s5(z, win, wout, wglu, a_rows, aseg_rows, d_row, batch, seq):
    T, N = z.shape
    blk = S5_BLOCK
    nb = seq // blk
    u_col = (4 * MLSTM_WIDTH) // LANES
    tile = lambda shape: pl.BlockSpec((None,) + shape, lambda j, c: (j,) + (0,) * len(shape))
    ys = pl.pallas_call(
        _s5_kernel,
        grid=(S5_TILES, nb),
        in_specs=[
            pl.BlockSpec((batch, blk, LANES), lambda j, c: (0, c, u_col + j)),
            tile(win.shape[1:]), tile(wout.shape[1:]), tile(wglu.shape[1:]),
            tile(a_rows.shape[1:]), tile(aseg_rows.shape[1:]),
            pl.BlockSpec((1, LANES), lambda j, c: (0, j)),
        ],
        out_specs=pl.BlockSpec((batch, blk, LANES), lambda j, c: (0, c, j)),
        out_shape=jax.ShapeDtypeStruct((batch, seq, S5_WIDTH), F32),
        scratch_shapes=[
            pltpu.VMEM((batch, blk, LANES), F32),
            pltpu.VMEM((batch, blk, 2 * S5_TILE_STATE), F32),
            pltpu.VMEM((batch, blk, 2 * S5_TILE_STATE), F32),
            pltpu.VMEM((batch, blk, LANES), F32),
            pltpu.VMEM((batch, 1, 2 * S5_TILE_STATE), F32),
        ],
        compiler_params=_params("parallel", "arbitrary"),
        name="s5",
    )(z.reshape(batch, seq, N), win, wout, wglu, a_rows, aseg_rows, d_row)
    return ys.reshape(T, S5_WIDTH)


def _s5_weights(lam_re, lam_im, log_dt, b_re, b_im, c_re, c_im, d, glu_w):
    G, P, Hg = S5_GROUPS, S5_STATE, S5_GROUP
    nt, gl = S5_TILES, G // S5_TILES
    dt = jnp.exp(log_dt.astype(F32))[:, None]
    mag = jnp.exp(lam_re * dt)
    ar = mag * jnp.cos(lam_im * dt)
    ai = mag * jnp.sin(lam_im * dt)
    den = lam_re * lam_re + lam_im * lam_im
    fr = ((ar - 1.0) * lam_re + ai * lam_im) / den
    fi = (ai * lam_re - (ar - 1.0) * lam_im) / den
    bbr = fr[..., None] * b_re - fi[..., None] * b_im
    bbi = fr[..., None] * b_im + fi[..., None] * b_re
    eye = jnp.eye(gl, dtype=F32)

    def tiles(v):
        return v.reshape((nt, gl) + v.shape[1:])

    bb = jnp.stack([tiles(bbr), tiles(bbi)], axis=2)
    win = jnp.einsum('tgrph,gk->tghrkp', bb, eye).reshape(nt, gl * Hg, 2 * gl * P)
    cc = jnp.stack([tiles(c_re), -tiles(c_im)], axis=2)
    wout = jnp.einsum('tgrhp,gk->trkpgh', cc, eye).reshape(nt, 2 * gl * P, gl * Hg)
    gw = tiles(glu_w).reshape(nt, gl, Hg, 2, Hg)
    wglu = jnp.einsum('tghrk,gq->tqhrgk', gw, eye).reshape(nt, gl * Hg, 2 * gl * Hg)

    def rows(re, im):
        return jnp.concatenate([tiles(re).reshape(nt, 1, gl * P), tiles(im).reshape(nt, 1, gl * P)], axis=-1)

    a_rows = jnp.broadcast_to(rows(ar, ai), (nt, SUBLANES, 2 * gl * P))
    pr, pi = ar, ai
    for _ in range(int(math.log2(S5_SEG))):
        pr, pi = pr * pr - pi * pi, 2.0 * pr * pi
    aseg_rows = rows(pr, pi)
    d_row = d.reshape(1, G * Hg).astype(F32)
    return win.astype(BF16), wout.astype(BF16), wglu.astype(BF16), a_rows, aseg_rows, d_row


def _outproj_kernel(hm_ref, ys_ref, w1_ref, w2_ref, x_ref, g_ref, h1_ref, xn_ref):
    mix = (jnp.dot(hm_ref[...], w1_ref[...], preferred_element_type=F32)
           + jnp.dot(ys_ref[...].astype(BF16), w2_ref[...], preferred_element_type=F32))
    h1 = x_ref[...] + mix
    h1_ref[...] = h1
    xn_ref[...] = (h1 * _rms_scale(h1) * g_ref[...]).astype(BF16)


def _outproj(hm, ys, w, x2d, g_row, tm=512):
    T, D = x2d.shape
    W = hm.shape[1]
    row = lambda w: pl.BlockSpec((tm, w), lambda i: (i, 0))
    half = lambda k: pl.BlockSpec((W, D), lambda i: (k, 0))
    return pl.pallas_call(
        _outproj_kernel,
        grid=(T // tm,),
        in_specs=[row(W), row(W), half(0), half(1), row(D), pl.BlockSpec((1, D), lambda i: (0, 0))],
        out_specs=[row(D), row(D)],
        out_shape=[jax.ShapeDtypeStruct((T, D), F32), jax.ShapeDtypeStruct((T, D), BF16)],
        compiler_params=_params("parallel"),
        name="outproj",
    )(hm, ys, w, w, x2d, g_row)


ROUTE_GROUP = 1024
ROUTE_CHUNK = LANES
GATE_PIECE = 4
GATE_LAG = 1
assert PEER_HEADS == SUBLANES


def _sort_pairs(n):
    pairs = []
    p = 1
    while p < n:
        k = p
        while k >= 1:
            for j in range(k % p, n - k, 2 * k):
                for i in range(min(k, n - j - k)):
                    if (i + j) // (2 * p) == (i + j + k) // (2 * p):
                        pairs.append((i + j, i + j + k))
            k //= 2
        p *= 2
    return pairs


def _first_second(a, b):
    (av, ai), (bv, bi) = a, b
    tie = av == bv
    a_first = jnp.where(tie, bi, av) > jnp.where(tie, ai, bv)
    return ((jnp.maximum(av, bv), jnp.where(a_first, ai, bi)),
            (jnp.minimum(av, bv), jnp.where(a_first, bi, ai)))


def _sorted_group(items):
    items = list(items)
    for i, j in _sort_pairs(len(items)):
        items[i], items[j] = _first_second(items[i], items[j])
    return items


def _merge_top(a, b):
    n = len(a)
    c = [_first_second(a[v], b[n - 1 - v])[0] for v in range(n)]
    j = n // 2
    while j >= 1:
        for i in range(n):
            if (i & j) == 0:
                c[i], c[i + j] = _first_second(c[i], c[i + j])
        j //= 2
    return c


def _top_sorted(groups, stage=lambda items: items):
    groups = list(groups)
    while len(groups) > 1:
        groups = [_merge_top(stage(groups[i]), groups[i + 1]) for i in range(0, len(groups), 2)]
    return groups[0]


def _product_top(top1, top2, stage=lambda items: items):
    K = PEER_TOPK
    pairs = [(r1, r2) for r1 in range(K) for r2 in range(K) if (r1 + 1) * (r2 + 1) <= K]
    cands = [(top1[r1][0] + top2[r2][0], jnp.full_like(top1[0][0], float(r1 * K + r2))) for r1, r2 in pairs]
    pad = (jnp.full_like(top1[0][0], -jnp.inf), jnp.full_like(top1[0][0], float(K * K)))
    cands += [pad] * (-len(cands) % K)
    groups = [cands[:K]]
    for g in range(K, len(cands), K):
        groups.append(_sorted_group(stage(cands[g:g + K])))
    best = _top_sorted(groups, stage)
    top_s = [v for v, _ in best]
    e = [jnp.exp(v - top_s[0]) for v in top_s]
    z = functools.reduce(lambda x, y: x + y, e)
    gates, i1s, i2s = [], [], []
    for r in range(K):
        pos = best[r][1]
        r1 = jnp.floor(pos * np.float32(1.0 / K))
        r2 = pos - r1 * np.float32(K)
        i1, i2 = top1[0][1], top2[0][1]
        for a in range(1, K):
            i1 = jnp.where(r1 == float(a), top1[a][1], i1)
            i2 = jnp.where(r2 == float(a), top2[a][1], i2)
        gates.append(e[r] / z)
        i1s.append(i1)
        i2s.append(i2)
    return i1s, i2s, gates


def _route_gates_kernel(x_ref, wq_ref, sk_ref, g_ref, q_ref, sc_ref, slot_ref, tok_ref):
    K, n, half, H = PEER_TOPK, PEER_NKEYS, PEER_QDIM // 2, PEER_HEADS
    s = pl.program_id(0)
    n_chunks = pl.num_programs(0) - 1
    per_group = ROUTE_GROUP // ROUTE_CHUNK
    cur = s % 2

    @pl.when(s == 0)
    def _():
        slot_ref[...] = jnp.zeros_like(slot_ref)

    @pl.when(jnp.logical_and(s % per_group == 0, s < n_chunks))
    def _():
        q = jnp.dot(x_ref[...], wq_ref[...], preferred_element_type=F32)
        for hh in range(H):
            q_ref[hh] = q[:, hh * PEER_QDIM:(hh + 1) * PEER_QDIM].astype(BF16)

    for a in range(3):
        tok_ref[a] = slot_ref[1 - cur, a].T
    keys = lax.broadcasted_iota(jnp.int32, (n, n), 0).astype(F32)
    pieces = iter(range(ROUTE_CHUNK // GATE_PIECE))
    done = []

    def gate_piece():
        k = next(pieces, None)
        if k is None:
            return
        for t in range(k * GATE_PIECE, (k + 1) * GATE_PIECE):
            at = jnp.where(keys == tok_ref[0, t:t + 1, :], 1.0, 0.0).astype(BF16)
            bt = jnp.where(keys == tok_ref[1, t:t + 1, :], tok_ref[2, t:t + 1, :], 0.0).astype(BF16)
            g = lax.dot_general(at, bt, NT_DIMS, preferred_element_type=F32)
            g_ref[t] = g
        done.append(g[0:SUBLANES, :] * 0.0)

    def stage(items):
        gate_piece()
        if len(done) > GATE_LAG:
            (v, i), rest = items[0], list(items[1:])
            return [(v + done[len(done) - 1 - GATE_LAG], i)] + rest
        return items

    row0 = pl.multiple_of((s % per_group) * ROUTE_CHUNK, ROUTE_CHUNK)
    tops = []
    for p in range(2):
        for h in range(H):
            qc = q_ref[h, pl.ds(row0, ROUTE_CHUNK), p * half:(p + 1) * half]
            sc_ref[p, pl.ds(h, n, stride=H), :] = lax.dot_general(
                sk_ref[p], qc, NT_DIMS, preferred_element_type=F32)
        items = [(sc_ref[p, k * SUBLANES:(k + 1) * SUBLANES, :], jnp.full((SUBLANES, LANES), float(k), F32))
                 for k in range(n)]
        groups = [_sorted_group(stage(items[g:g + K])) for g in range(0, n, K)]
        tops.append(_top_sorted(groups, stage))
    i1s, i2s, gates = _product_top(*tops, stage)
    for _ in range(ROUTE_CHUNK // GATE_PIECE):
        gate_piece()
    for a, vals in enumerate((i1s, i2s, gates)):
        for r in range(K):
            slot_ref[cur, a, pl.ds(r, H, stride=K), :] = vals[r]


def _route_gates(xn2, wq, subkeys):
    T, D = xn2.shape
    n, slots = PEER_NKEYS, PEER_HEADS * PEER_TOPK
    n_chunks = T // ROUTE_CHUNK
    per_group = ROUTE_GROUP // ROUTE_CHUNK
    n_groups = T // ROUTE_GROUP
    once = pl.Buffered(1)
    return pl.pallas_call(
        _route_gates_kernel,
        grid=(n_chunks + 1,),
        in_specs=[
            pl.BlockSpec((ROUTE_GROUP, D), lambda s: (jnp.minimum(s // per_group, n_groups - 1), 0)),
            pl.BlockSpec(wq.shape, lambda s: (0, 0), pipeline_mode=once),
            pl.BlockSpec(subkeys.shape, lambda s: (0, 0, 0)),
        ],
        out_specs=pl.BlockSpec((ROUTE_CHUNK, n, n), lambda s: (jnp.maximum(s - 1, 0), 0, 0)),
        out_shape=jax.ShapeDtypeStruct((T, n, n), F32),
        scratch_shapes=[pltpu.VMEM((PEER_HEADS, ROUTE_GROUP, PEER_QDIM), BF16),
                        pltpu.VMEM((2, n * SUBLANES, LANES), F32),
                        pltpu.VMEM((2, 3, slots, ROUTE_CHUNK), F32),
                        pltpu.VMEM((3, ROUTE_CHUNK, slots), F32)],
        compiler_params=_params("arbitrary"),
        name="peer_route_gates",
    )(xn2, wq, subkeys)


EXPERT_G_ROWS = SUBLANES


def _experts_kernel(x_ref, u_ref, v_ref, g_ref, h1_ref, fg_ref, y_ref):
    e, k = pl.program_id(1), pl.program_id(2)
    first = jnp.logical_and(e == 0, k == 0)
    last = jnp.logical_and(e == pl.num_programs(1) - 1, k == pl.num_programs(2) - 1)

    @pl.when(first)
    def _():
        y_ref[...] = jnp.zeros_like(y_ref)

    u = u_ref[...].astype(BF16)
    v = v_ref[...].astype(BF16)
    act = _gelu(lax.dot_general(x_ref[...], u, NT_DIMS, preferred_element_type=F32))
    n = PEER_NKEYS
    tm, rows, _ = g_ref.shape
    nc = u_ref.shape[0] // n
    g_rows = g_ref.reshape(tm * rows, n)
    w = jnp.concatenate([(g_rows[pl.ds(k * nc + c, tm, stride=rows), :] * act[:, c * n:(c + 1) * n]).astype(BF16)
                         for c in range(nc)], axis=1)
    y_ref[...] += jnp.dot(w, v, preferred_element_type=F32)

    @pl.when(last)
    def _():
        h2 = h1_ref[...] + y_ref[...]
        y_ref[...] = h2 * _rms_scale(h2) * fg_ref[...]


def _experts(xn2, u_tab, v_tab, g3, h1, fg_row, tm=1024, te=512):
    T, D = xn2.shape
    E = u_tab.shape[0]
    n = PEER_NKEYS
    sub = EXPERT_G_ROWS * n // te
    once = pl.Buffered(1)
    return pl.pallas_call(
        _experts_kernel,
        grid=(T // tm, E // (te * sub), sub),
        in_specs=[
            pl.BlockSpec((tm, D), lambda i, e, k: (i, 0), pipeline_mode=once),
            pl.BlockSpec((te, D), lambda i, e, k: (e * sub + k, 0)),
            pl.BlockSpec((te, D), lambda i, e, k: (e * sub + k, 0)),
            pl.BlockSpec((tm, EXPERT_G_ROWS, n), lambda i, e, k: (i, e, 0)),
            pl.BlockSpec((tm, D), lambda i, e, k: (i, 0), pipeline_mode=once),
            pl.BlockSpec((1, D), lambda i, e, k: (0, 0)),
        ],
        out_specs=pl.BlockSpec((tm, D), lambda i, e, k: (i, 0), pipeline_mode=once),
        out_shape=jax.ShapeDtypeStruct((T, D), F32),
        compiler_params=_params("parallel", "arbitrary", "arbitrary"),
        name="peer_experts",
    )(xn2, u_tab, v_tab, g3, h1, fg_row)


def kernel(x, norm1_g, w_in, b_gates, conv_qk_w, mlstm_norm_g, s5_lambda_re, s5_lambda_im, s5_log_dt,
           s5_b_re, s5_b_im, s5_c_re, s5_c_im, s5_d, s5_glu_w, w_out, norm2_g, peer_wq, peer_subkeys,
           peer_u, peer_v, final_g):
    B, S, D = x.shape
    W, H = MLSTM_WIDTH, MLSTM_HEADS
    depth = norm1_g.shape[0]
    h = x.reshape(B * S, D)
    for l in range(depth):
        wl = w_in[l].astype(BF16)
        n_gate = 2 * H
        w_u = wl[:, 4 * W + n_gate:]
        w_gates = jnp.pad(wl[:, 4 * W:4 * W + n_gate], ((0, 0), (0, LANES - n_gate)))
        bias_row = jnp.pad(b_gates[l].astype(F32), (0, LANES - n_gate)).reshape(1, LANES)
        s5w = _s5_weights(s5_lambda_re[l].astype(F32), s5_lambda_im[l].astype(F32), s5_log_dt[l],
                          s5_b_re[l].astype(F32), s5_b_im[l].astype(F32), s5_c_re[l].astype(F32),
                          s5_c_im[l].astype(F32), s5_d[l], s5_glu_w[l].astype(F32))
        wo = w_out[l].astype(BF16)

        z, gates = _inproj(h, norm1_g[l].reshape(1, D).astype(F32), wl, w_u, w_gates)
        hm = _mlstm(z, gates, bias_row, conv_qk_w[l].astype(F32),
                    mlstm_norm_g[l].reshape(1, W).astype(F32), B, S)
        ys = _s5(z, *s5w, B, S)
        h1, xn2 = _outproj(hm, ys, wo, h, norm2_g[l].reshape(1, D).astype(F32))

        g3 = _route_gates(xn2, peer_wq[l].astype(BF16), peer_subkeys[l].astype(BF16))
        assert l == depth - 1, "kernel is specialised to a single layer followed by the final norm"
        h = _experts(xn2, peer_u[l], peer_v[l], g3, h1,
                     final_g.reshape(1, D).astype(F32))
    return h.reshape(B, S, D)
```

```python
import functools
import math

import jax
import jax.numpy as jnp
import numpy as np
from jax import lax
from jax.experimental import pallas as pl
from jax.experimental.pallas import tpu as pltpu

F32 = jnp.float32
BF16 = jnp.bfloat16

D_MODEL = 2048
MLSTM_WIDTH = 1024
MLSTM_HEADS = 4
HEAD_DIM = 256
CHUNK = 128
CONV_WIDTH = 4
S5_WIDTH = 1024
S5_GROUP = 16
S5_GROUPS = 64
S5_STATE = 64
PEER_HEADS = 8
PEER_NKEYS = 128
PEER_EXPERTS = PEER_NKEYS * PEER_NKEYS
PEER_TOPK = 16
PEER_QDIM = 256
RMS_EPS = 1e-6

LANES = 128
SUBLANES = 8
VMEM_LIMIT = 56 * 1024 * 1024

NT_DIMS = (((1,), (1,)), ((), ()))


def _params(*sem):
    return pltpu.CompilerParams(dimension_semantics=sem, vmem_limit_bytes=VMEM_LIMIT)


def _rms_scale(v):
    return lax.rsqrt(jnp.mean(v * v, axis=-1, keepdims=True) + RMS_EPS)


def _gelu(v):
    return 0.5 * v * (1.0 + lax.erf(v * np.float32(math.sqrt(0.5))))


def _sigmoid(v):
    return 1.0 / (1.0 + jnp.exp(-v))


def _log_sigmoid(v):
    return jnp.minimum(v, 0.0) - jnp.log1p(jnp.exp(-jnp.abs(v)))


def _inproj_kernel(x_ref, g_ref, w_ref, wu_ref, wg_ref, z_ref, gates_ref, xn_ref):
    j = pl.program_id(1)
    last = pl.num_programs(1) - 1

    @pl.when(j == 0)
    def _():
        x = x_ref[...]
        xn_ref[...] = (x * _rms_scale(x) * g_ref[...]).astype(BF16)
        gates_ref[...] = lax.dot_general(xn_ref[...], wg_ref[...], NT_DIMS, preferred_element_type=F32)

    n_main = pl.num_programs(1) - S5_WIDTH // wu_ref.shape[0]

    @pl.when(j < n_main)
    def _():
        z_ref[...] = lax.dot_general(xn_ref[...], w_ref[...].astype(BF16), NT_DIMS, preferred_element_type=F32)

    @pl.when(j >= n_main)
    def _():
        z_ref[...] = lax.dot_general(xn_ref[...], wu_ref[...].astype(BF16), NT_DIMS, preferred_element_type=F32)


def _inproj(x2d, g_row, w_t, w_u_t, w_gates_t, tm=1024, tn=512):
    T, D = x2d.shape
    n_main = (4 * MLSTM_WIDTH) // tn
    n_u = w_u_t.shape[0] // tn
    return pl.pallas_call(
        _inproj_kernel,
        grid=(T // tm, n_main + n_u),
        in_specs=[
            pl.BlockSpec((tm, D), lambda i, j: (i, 0)),
            pl.BlockSpec((1, D), lambda i, j: (0, 0)),
            pl.BlockSpec((tn, D), lambda i, j: (jnp.minimum(j, n_main - 1), 0)),
            pl.BlockSpec((tn, D), lambda i, j: (jnp.maximum(j - n_main, 0), 0)),
            pl.BlockSpec((LANES, D), lambda i, j: (0, 0)),
        ],
        out_specs=[
            pl.BlockSpec((tm, tn), lambda i, j: (i, j)),
            pl.BlockSpec((tm, LANES), lambda i, j: (i, 0)),
        ],
        out_shape=[jax.ShapeDtypeStruct((T, (n_main + n_u) * tn), F32), jax.ShapeDtypeStruct((T, LANES), F32)],
        scratch_shapes=[pltpu.VMEM((tm, D), BF16)],
        compiler_params=_params("parallel", "arbitrary"),
        name="inproj",
    )(x2d, g_row, w_t, w_u_t, w_gates_t)


def _mlstm_kernel(q_ref, k_ref, v_ref, o_ref, gates_ref, bias_ref, convw_ref, ng_ref, hm_ref,
                  xe_ref, c_ref, n_ref, m_ref):
    L, H, Dh, W = CHUNK, MLSTM_HEADS, HEAD_DIM, MLSTM_WIDTH
    tail = SUBLANES

    @pl.when(pl.program_id(1) == 0)
    def _():
        xe_ref[0:tail, :] = jnp.zeros((tail, 2 * W), F32)
        c_ref[...] = jnp.zeros_like(c_ref)
        n_ref[...] = jnp.zeros_like(n_ref)
        m_ref[...] = jnp.zeros_like(m_ref)

    xe_ref[tail:tail + L, 0:W] = q_ref[...]
    xe_ref[tail:tail + L, W:2 * W] = k_ref[...]
    ext = xe_ref[0:tail + L, :].astype(BF16)
    pick_r = lax.broadcasted_iota(jnp.int32, ((CONV_WIDTH - 1) * L, tail + L), 0)
    pick_c = lax.broadcasted_iota(jnp.int32, ((CONV_WIDTH - 1) * L, tail + L), 1)
    pick = jnp.where(pick_c == (pick_r % L) + (tail - (CONV_WIDTH - 1)) + pick_r // L, 1.0, 0.0).astype(BF16)
    shifted = jnp.dot(pick, ext, preferred_element_type=F32)
    conv = convw_ref[CONV_WIDTH - 1:CONV_WIDTH, :] * xe_ref[tail:tail + L, :]
    for j in range(CONV_WIDTH - 1):
        conv = conv + convw_ref[j:j + 1, :] * shifted[j * L:(j + 1) * L, :]
    xe_ref[0:tail, :] = xe_ref[L:L + tail, :]
    qk = conv * _sigmoid(conv)

    g = gates_ref[...] + bias_ref[...]
    g_t = g.T
    row = lax.broadcasted_iota(jnp.int32, (L, L), 0)
    col = lax.broadcasted_iota(jnp.int32, (L, L), 1)
    causal = col <= row
    tri = jnp.where(causal, 1.0, 0.0).astype(F32)
    tri_t = jnp.where(row <= col, 1.0, 0.0).astype(F32)
    hi = lax.Precision.HIGHEST
    a_cols = jnp.dot(tri, _log_sigmoid(g), precision=hi, preferred_element_type=F32)
    a_rows = jnp.dot(_log_sigmoid(g_t), tri_t, precision=hi, preferred_element_type=F32)

    scale = np.float32(Dh ** -0.5)
    for h in range(H):
        a_c = a_cols[:, H + h:H + h + 1]
        a_r = a_rows[H + h:H + h + 1, :]
        li_c = g[:, h:h + 1]
        li_r = g_t[h:h + 1, :]
        a_tot = a_r[:, L - 1:L]
        m0 = m_ref[h]
        c0 = c_ref[h]
        n0 = n_ref[h]

        q = qk[:, h * Dh:(h + 1) * Dh] * scale
        k = qk[:, W + h * Dh:W + (h + 1) * Dh]
        v = v_ref[:, h * Dh:(h + 1) * Dh]
        qb, kb, vb = q.astype(BF16), k.astype(BF16), v.astype(BF16)

        dmat = jnp.where(causal, a_c - a_r + li_r, -jnp.inf)
        m_inter = a_c + m0
        m_t = jnp.maximum(m_inter, jnp.max(dmat, axis=-1, keepdims=True))
        wm = jnp.exp(dmat - m_t)
        s_inter = jnp.exp(m_inter - m_t)
        s = lax.dot_general(qb, kb, NT_DIMS, preferred_element_type=F32) * wm
        num = (jnp.dot(s.astype(BF16), vb, preferred_element_type=F32)
               + s_inter * jnp.dot(qb, c0.astype(BF16), preferred_element_type=F32))
        den = (jnp.sum(s, axis=-1, keepdims=True)
               + s_inter * jnp.sum(q * n0, axis=-1, keepdims=True))
        hh = num / jnp.maximum(jnp.abs(den), jnp.exp(-m_t))
        hh = hh * _rms_scale(hh)
        og = _sigmoid(o_ref[:, h * Dh:(h + 1) * Dh])
        hm_ref[:, h * Dh:(h + 1) * Dh] = (hh * ng_ref[:, h * Dh:(h + 1) * Dh] * og).astype(BF16)

        g_r = a_tot - a_r + li_r
        g_c = a_tot - a_c + li_c
        m_loc = jnp.max(g_r, axis=-1, keepdims=True)
        kw = k * jnp.exp(g_c - m_loc)
        d_c = jnp.dot(kw.T.astype(BF16), vb, preferred_element_type=F32)
        d_n = jnp.sum(kw, axis=0, keepdims=True)
        m_new = jnp.maximum(a_tot + m0, m_loc)
        s_old = jnp.exp(a_tot + m0 - m_new)
        s_new = jnp.exp(m_loc - m_new)
        c_ref[h] = s_old * c0 + s_new * d_c
        n_ref[h] = s_old * n0 + s_new * d_n
        m_ref[h] = m_new


def _mlstm(z, gates, bias_row, conv_w, norm_g_row, batch, seq):
    T = z.shape[0]
    L, W = CHUNK, MLSTM_WIDTH
    nc = seq // L
    blk = lambda col: pl.BlockSpec((L, W), lambda b, c, col=col: (b * nc + c, col))
    return pl.pallas_call(
        _mlstm_kernel,
        grid=(batch, nc),
        in_specs=[
            blk(0), blk(1), blk(2), blk(3),
            pl.BlockSpec((L, LANES), lambda b, c: (b * nc + c, 0)),
            pl.BlockSpec((1, LANES), lambda b, c: (0, 0)),
            pl.BlockSpec((CONV_WIDTH, 2 * W), lambda b, c: (0, 0)),
            pl.BlockSpec((1, W), lambda b, c: (0, 0)),
        ],
        out_specs=pl.BlockSpec((L, W), lambda b, c: (b * nc + c, 0)),
        out_shape=jax.ShapeDtypeStruct((T, W), BF16),
        scratch_shapes=[
            pltpu.VMEM((L + 2 * SUBLANES, 2 * W), F32),
            pltpu.VMEM((MLSTM_HEADS, HEAD_DIM, HEAD_DIM), F32),
            pltpu.VMEM((MLSTM_HEADS, 1, HEAD_DIM), F32),
            pltpu.VMEM((MLSTM_HEADS, 1, 1), F32),
        ],
        compiler_params=_params("parallel", "arbitrary"),
        name="mlstm",
    )(z, z, z, z, gates, bias_row, conv_w, norm_g_row)


S5_BLOCK = 256
S5_SEG = S5_BLOCK // SUBLANES
S5_TILES = S5_WIDTH // LANES
S5_TILE_STATE = 8 * S5_STATE


def _s5_kernel(u_ref, win_ref, wout_ref, wglu_ref, a_ref, aseg_ref, d_ref, ys_ref,
               up_ref, x_ref, xs_ref, yp_ref, carry_ref):
    nseg, seg, half = SUBLANES, S5_SEG, S5_TILE_STATE
    chains = range(u_ref.shape[0])

    @pl.when(pl.program_id(1) == 0)
    def _():
        carry_ref[...] = jnp.zeros_like(carry_ref)

    def cmul_add(ar, ai, sr, si, xr, xi):
        return ar * sr - ai * si + xr, ar * si + ai * sr + xi

    def x_rows(b, i):
        rows = slice(nseg * i, nseg * (i + 1))
        return x_ref[b, rows, 0:half], x_ref[b, rows, half:2 * half]

    for i in range(seg):
        for b in chains:
            up_ref[b, nseg * i:nseg * (i + 1), :] = u_ref[b, pl.ds(i, nseg, stride=seg), :]
    for b in chains:
        x_ref[b] = jnp.dot(up_ref[b].astype(BF16), win_ref[...], preferred_element_type=F32)
    ar = a_ref[:, 0:half]
    ai = a_ref[:, half:2 * half]

    fin = [(jnp.zeros((nseg, half), F32), jnp.zeros((nseg, half), F32)) for _ in chains]
    for i in range(seg):
        for b in chains:
            fin[b] = cmul_add(ar, ai, *fin[b], *x_rows(b, i))

    lr = aseg_ref[:, 0:half]
    lim = aseg_ref[:, half:2 * half]
    state = []
    for b in chains:
        cr = carry_ref[b, :, 0:half]
        ci = carry_ref[b, :, half:2 * half]
        crs, cis = [], []
        for r in range(nseg):
            crs.append(cr)
            cis.append(ci)
            cr, ci = cmul_add(lr, lim, cr, ci, fin[b][0][r:r + 1, :], fin[b][1][r:r + 1, :])
        carry_ref[b, :, 0:half] = cr
        carry_ref[b, :, half:2 * half] = ci
        state.append((jnp.concatenate(crs, axis=0), jnp.concatenate(cis, axis=0)))

    for i in range(seg):
        rows = slice(nseg * i, nseg * (i + 1))
        for b in chains:
            state[b] = cmul_add(ar, ai, *state[b], *x_rows(b, i))
            xs_ref[b, rows, 0:half] = state[b][0]
            xs_ref[b, rows, half:2 * half] = state[b][1]

    for b in chains:
        y = jnp.dot(xs_ref[b].astype(BF16), wout_ref[...], preferred_element_type=F32)
        y = y + d_ref[...] * up_ref[b]
        ab = jnp.dot(_gelu(y).astype(BF16), wglu_ref[...], preferred_element_type=F32)
        yp_ref[b] = ab[:, 0:LANES] * _sigmoid(ab[:, LANES:2 * LANES])
    for i in range(seg):
        for b in chains:
            ys_ref[b, pl.ds(i, nseg, stride=seg), :] = yp_ref[b, nseg * i:nseg * (i + 1), :]


def _s5(z, win, wout, wglu, a_rows, aseg_rows, d_row, batch, seq):
    T, N = z.shape
    blk = S5_BLOCK
    nb = seq // blk
    u_col = (4 * MLSTM_WIDTH) // LANES
    tile = lambda shape: pl.BlockSpec((None,) + shape, lambda j, c: (j,) + (0,) * len(shape))
    ys = pl.pallas_call(
        _s5_kernel,
        grid=(S5_TILES, nb),
        in_specs=[
            pl.BlockSpec((batch, blk, LANES), lambda j, c: (0, c, u_col + j)),
            tile(win.shape[1:]), tile(wout.shape[1:]), tile(wglu.shape[1:]),
            tile(a_rows.shape[1:]), tile(aseg_rows.shape[1:]),
            pl.BlockSpec((1, LANES), lambda j, c: (0, j)),
        ],
        out_specs=pl.BlockSpec((batch, blk, LANES), lambda j, c: (0, c, j)),
        out_shape=jax.ShapeDtypeStruct((batch, seq, S5_WIDTH), F32),
        scratch_shapes=[
            pltpu.VMEM((batch, blk, LANES), F32),
            pltpu.VMEM((batch, blk, 2 * S5_TILE_STATE), F32),
            pltpu.VMEM((batch, blk, 2 * S5_TILE_STATE), F32),
            pltpu.VMEM((batch, blk, LANES), F32),
            pltpu.VMEM((batch, 1, 2 * S5_TILE_STATE), F32),
        ],
        compiler_params=_params("parallel", "arbitrary"),
        name="s5",
    )(z.reshape(batch, seq, N), win, wout, wglu, a_rows, aseg_rows, d_row)
    return ys.reshape(T, S5_WIDTH)


def _s5_weights(lam_re, lam_im, log_dt, b_re, b_im, c_re, c_im, d, glu_w):
    G, P, Hg = S5_GROUPS, S5_STATE, S5_GROUP
    nt, gl = S5_TILES, G // S5_TILES
    dt = jnp.exp(log_dt.astype(F32))[:, None]
    mag = jnp.exp(lam_re * dt)
    ar = mag * jnp.cos(lam_im * dt)
    ai = mag * jnp.sin(lam_im * dt)
    den = lam_re * lam_re + lam_im * lam_im
    fr = ((ar - 1.0) * lam_re + ai * lam_im) / den
    fi = (ai * lam_re - (ar - 1.0) * lam_im) / den
    bbr = fr[..., None] * b_re - fi[..., None] * b_im
    bbi = fr[..., None] * b_im + fi[..., None] * b_re
    eye = jnp.eye(gl, dtype=F32)

    def tiles(v):
        return v.reshape((nt, gl) + v.shape[1:])

    bb = jnp.stack([tiles(bbr), tiles(bbi)], axis=2)
    win = jnp.einsum('tgrph,gk->tghrkp', bb, eye).reshape(nt, gl * Hg, 2 * gl * P)
    cc = jnp.stack([tiles(c_re), -tiles(c_im)], axis=2)
    wout = jnp.einsum('tgrhp,gk->trkpgh', cc, eye).reshape(nt, 2 * gl * P, gl * Hg)
    gw = tiles(glu_w).reshape(nt, gl, Hg, 2, Hg)
    wglu = jnp.einsum('tghrk,gq->tqhrgk', gw, eye).reshape(nt, gl * Hg, 2 * gl * Hg)

    def rows(re, im):
        return jnp.concatenate([tiles(re).reshape(nt, 1, gl * P), tiles(im).reshape(nt, 1, gl * P)], axis=-1)

    a_rows = jnp.broadcast_to(rows(ar, ai), (nt, SUBLANES, 2 * gl * P))
    pr, pi = ar, ai
    for _ in range(int(math.log2(S5_SEG))):
        pr, pi = pr * pr - pi * pi, 2.0 * pr * pi
    aseg_rows = rows(pr, pi)
    d_row = d.reshape(1, G * Hg).astype(F32)
    return win.astype(BF16), wout.astype(BF16), wglu.astype(BF16), a_rows, aseg_rows, d_row


def _outproj_kernel(hm_ref, ys_ref, w1_ref, w2_ref, x_ref, g_ref, h1_ref, xn_ref):
    mix = (jnp.dot(hm_ref[...], w1_ref[...].astype(BF16), preferred_element_type=F32)
           + jnp.dot(ys_ref[...].astype(BF16), w2_ref[...].astype(BF16), preferred_element_type=F32))
    h1 = x_ref[...] + mix
    h1_ref[...] = h1
    xn_ref[...] = (h1 * _rms_scale(h1) * g_ref[...]).astype(BF16)


def _outproj(hm, ys, w, x2d, g_row, tm=512):
    T, D = x2d.shape
    W = hm.shape[1]
    row = lambda w: pl.BlockSpec((tm, w), lambda i: (i, 0))
    half = lambda k: pl.BlockSpec((W, D), lambda i: (k, 0), pipeline_mode=pl.Buffered(1))
    return pl.pallas_call(
        _outproj_kernel,
        grid=(T // tm,),
        in_specs=[row(W), row(W), half(0), half(1), row(D), pl.BlockSpec((1, D), lambda i: (0, 0))],
        out_specs=[row(D), row(D)],
        out_shape=[jax.ShapeDtypeStruct((T, D), F32), jax.ShapeDtypeStruct((T, D), BF16)],
        compiler_params=_params("parallel"),
        name="outproj",
    )(hm, ys, w, w, x2d, g_row)


ROUTE_GROUP = 1024
ROUTE_CHUNK = LANES
GATE_PIECE = 4
GATE_LAG = 1
assert PEER_HEADS == SUBLANES


def _sort_pairs(n):
    pairs = []
    p = 1
    while p < n:
        k = p
        while k >= 1:
            for j in range(k % p, n - k, 2 * k):
                for i in range(min(k, n - j - k)):
                    if (i + j) // (2 * p) == (i + j + k) // (2 * p):
                        pairs.append((i + j, i + j + k))
            k //= 2
        p *= 2
    return pairs


def _first_second(a, b):
    (av, ai), (bv, bi) = a, b
    tie = av == bv
    a_first = jnp.where(tie, bi, av) > jnp.where(tie, ai, bv)
    return ((jnp.maximum(av, bv), jnp.where(a_first, ai, bi)),
            (jnp.minimum(av, bv), jnp.where(a_first, bi, ai)))


def _sorted_group(items):
    items = list(items)
    for i, j in _sort_pairs(len(items)):
        items[i], items[j] = _first_second(items[i], items[j])
    return items


def _merge_top(a, b):
    n = len(a)
    c = [_first_second(a[v], b[n - 1 - v])[0] for v in range(n)]
    j = n // 2
    while j >= 1:
        for i in range(n):
            if (i & j) == 0:
                c[i], c[i + j] = _first_second(c[i], c[i + j])
        j //= 2
    return c


def _top_sorted(groups, stage=lambda items: items):
    groups = list(groups)
    while len(groups) > 1:
        groups = [_merge_top(stage(groups[i]), groups[i + 1]) for i in range(0, len(groups), 2)]
    return groups[0]


def _product_top(top1, top2, stage=lambda items: items):
    K = PEER_TOPK
    pairs = [(r1, r2) for r1 in range(K) for r2 in range(K) if (r1 + 1) * (r2 + 1) <= K]
    cands = [(top1[r1][0] + top2[r2][0], jnp.full_like(top1[0][0], float(r1 * K + r2))) for r1, r2 in pairs]
    pad = (jnp.full_like(top1[0][0], -jnp.inf), jnp.full_like(top1[0][0], float(K * K)))
    cands += [pad] * (-len(cands) % K)
    groups = [cands[:K]]
    for g in range(K, len(cands), K):
        groups.append(_sorted_group(stage(cands[g:g + K])))
    best = _top_sorted(groups, stage)
    top_s = [v for v, _ in best]
    e = [jnp.exp(v - top_s[0]) for v in top_s]
    z = functools.reduce(lambda x, y: x + y, e)
    gates, i1s, i2s = [], [], []
    for r in range(K):
        pos = best[r][1]
        r1 = jnp.floor(pos * np.float32(1.0 / K))
        r2 = pos - r1 * np.float32(K)
        i1, i2 = top1[0][1], top2[0][1]
        for a in range(1, K):
            i1 = jnp.where(r1 == float(a), top1[a][1], i1)
            i2 = jnp.where(r2 == float(a), top2[a][1], i2)
        gates.append(e[r] / z)
        i1s.append(i1)
        i2s.append(i2)
    return i1s, i2s, gates


def _route_gates_kernel(x_ref, wq_ref, sk_ref, g_ref, q_ref, sc_ref, slot_ref, tok_ref):
    K, n, half, H = PEER_TOPK, PEER_NKEYS, PEER_QDIM // 2, PEER_HEADS
    s = pl.program_id(0)
    n_chunks = pl.num_programs(0) - 1
    per_group = ROUTE_GROUP // ROUTE_CHUNK
    cur = s % 2

    @pl.when(s == 0)
    def _():
        slot_ref[...] = jnp.zeros_like(slot_ref)

    @pl.when(jnp.logical_and(s % per_group == 0, s < n_chunks))
    def _():
        q = jnp.dot(x_ref[...], wq_ref[...], preferred_element_type=F32)
        for hh in range(H):
            q_ref[hh] = q[:, hh * PEER_QDIM:(hh + 1) * PEER_QDIM].astype(BF16)

    for a in range(3):
        tok_ref[a] = slot_ref[1 - cur, a].T
    keys = lax.broadcasted_iota(jnp.int32, (n, n), 0).astype(F32)
    pieces = iter(range(ROUTE_CHUNK // GATE_PIECE))
    done = []

    def gate_piece():
        k = next(pieces, None)
        if k is None:
            return
        for t in range(k * GATE_PIECE, (k + 1) * GATE_PIECE):
            at = jnp.where(keys == tok_ref[0, t:t + 1, :], 1.0, 0.0).astype(BF16)
            bt = jnp.where(keys == tok_ref[1, t:t + 1, :], tok_ref[2, t:t + 1, :], 0.0).astype(BF16)
            g = lax.dot_general(at, bt, NT_DIMS, preferred_element_type=F32)
            g_ref[t] = g
        done.append(g[0:SUBLANES, :] * 0.0)

    def stage(items):
        gate_piece()
        if len(done) > GATE_LAG:
            (v, i), rest = items[0], list(items[1:])
            return [(v + done[len(done) - 1 - GATE_LAG], i)] + rest
        return items

    row0 = pl.multiple_of((s % per_group) * ROUTE_CHUNK, ROUTE_CHUNK)
    tops = []
    for p in range(2):
        for h in range(H):
            qc = q_ref[h, pl.ds(row0, ROUTE_CHUNK), p * half:(p + 1) * half]
            sc_ref[p, pl.ds(h, n, stride=H), :] = lax.dot_general(
                sk_ref[p], qc, NT_DIMS, preferred_element_type=F32)
        items = [(sc_ref[p, k * SUBLANES:(k + 1) * SUBLANES, :], jnp.full((SUBLANES, LANES), float(k), F32))
                 for k in range(n)]
        groups = [_sorted_group(stage(items[g:g + K])) for g in range(0, n, K)]
        tops.append(_top_sorted(groups, stage))
    i1s, i2s, gates = _product_top(*tops, stage)
    for _ in range(ROUTE_CHUNK // GATE_PIECE):
        gate_piece()
    for a, vals in enumerate((i1s, i2s, gates)):
        for r in range(K):
            slot_ref[cur, a, pl.ds(r, H, stride=K), :] = vals[r]


def _route_gates(xn2, wq, subkeys):
    T, D = xn2.shape
    n, slots = PEER_NKEYS, PEER_HEADS * PEER_TOPK
    n_chunks = T // ROUTE_CHUNK
    per_group = ROUTE_GROUP // ROUTE_CHUNK
    n_groups = T // ROUTE_GROUP
    once = pl.Buffered(1)
    return pl.pallas_call(
        _route_gates_kernel,
        grid=(n_chunks + 1,),
        in_specs=[
            pl.BlockSpec((ROUTE_GROUP, D), lambda s: (jnp.minimum(s // per_group, n_groups - 1), 0)),
            pl.BlockSpec(wq.shape, lambda s: (0, 0), pipeline_mode=once),
            pl.BlockSpec(subkeys.shape, lambda s: (0, 0, 0)),
        ],
        out_specs=pl.BlockSpec((ROUTE_CHUNK, n, n), lambda s: (jnp.maximum(s - 1, 0), 0, 0)),
        out_shape=jax.ShapeDtypeStruct((T, n, n), F32),
        scratch_shapes=[pltpu.VMEM((PEER_HEADS, ROUTE_GROUP, PEER_QDIM), BF16),
                        pltpu.VMEM((2, n * SUBLANES, LANES), F32),
                        pltpu.VMEM((2, 3, slots, ROUTE_CHUNK), F32),
                        pltpu.VMEM((3, ROUTE_CHUNK, slots), F32)],
        compiler_params=_params("arbitrary"),
        name="peer_route_gates",
    )(xn2, wq, subkeys)


EXPERT_G_ROWS = SUBLANES


def _experts_kernel(x_ref, u_ref, v_ref, g_ref, h1_ref, fg_ref, y_ref, w_ref):
    j = pl.program_id(1)
    n_tiles = pl.num_programs(1) - 1
    n = PEER_NKEYS
    tm, rows, _ = g_ref.shape
    nc = u_ref.shape[0] // n
    sub = rows // nc

    def front():
        u = u_ref[...].astype(BF16)
        act = _gelu(lax.dot_general(x_ref[...], u, NT_DIMS, preferred_element_type=F32))
        g_rows = g_ref.reshape(tm * rows, n)
        k = j % sub
        w_ref[j % 2] = jnp.concatenate(
            [(g_rows[pl.ds(k * nc + c, tm, stride=rows), :] * act[:, c * n:(c + 1) * n]).astype(BF16)
             for c in range(nc)], axis=1)

    def back():
        y_ref[...] += jnp.dot(w_ref[(j + 1) % 2], v_ref[...].astype(BF16), preferred_element_type=F32)

    @pl.when(j == 0)
    def _():
        y_ref[...] = jnp.zeros_like(y_ref)
        front()

    @pl.when(jnp.logical_and(j > 0, j < n_tiles))
    def _():
        front()
        back()

    @pl.when(j == n_tiles)
    def _():
        back()
        h2 = h1_ref[...] + y_ref[...]
        y_ref[...] = h2 * _rms_scale(h2) * fg_ref[...]


def _experts(xn2, u_tab, v_tab, g3, h1, fg_row, tm=1024, te=512):
    T, D = xn2.shape
    E = u_tab.shape[0]
    n = PEER_NKEYS
    n_tiles = E // te
    sub = EXPERT_G_ROWS * n // te
    once = pl.Buffered(1)
    cur = lambda j: jnp.minimum(j, n_tiles - 1)
    prev = lambda j: jnp.maximum(j - 1, 0)
    return pl.pallas_call(
        _experts_kernel,
        grid=(T // tm, n_tiles + 1),
        in_specs=[
            pl.BlockSpec((tm, D), lambda i, j: (i, 0), pipeline_mode=once),
            pl.BlockSpec((te, D), lambda i, j: (cur(j), 0)),
            pl.BlockSpec((te, D), lambda i, j: (prev(j), 0)),
            pl.BlockSpec((tm, EXPERT_G_ROWS, n), lambda i, j: (i, cur(j) // sub, 0)),
            pl.BlockSpec((tm, D), lambda i, j: (i, 0), pipeline_mode=once),
            pl.BlockSpec((1, D), lambda i, j: (0, 0)),
        ],
        out_specs=pl.BlockSpec((tm, D), lambda i, j: (i, 0), pipeline_mode=once),
        out_shape=jax.ShapeDtypeStruct((T, D), F32),
        scratch_shapes=[pltpu.VMEM((2, tm, te), BF16)],
        compiler_params=_params("parallel", "arbitrary"),
        name="peer_experts",
    )(xn2, u_tab, v_tab, g3, h1, fg_row)


def kernel(x, norm1_g, w_in, b_gates, conv_qk_w, mlstm_norm_g, s5_lambda_re, s5_lambda_im, s5_log_dt,
           s5_b_re, s5_b_im, s5_c_re, s5_c_im, s5_d, s5_glu_w, w_out, norm2_g, peer_wq, peer_subkeys,
           peer_u, peer_v, final_g):
    B, S, D = x.shape
    W, H = MLSTM_WIDTH, MLSTM_HEADS
    depth = norm1_g.shape[0]
    h = x.reshape(B * S, D)
    for l in range(depth):
        wl = jnp.swapaxes(w_in[l], 0, 1)
        n_gate = 2 * H
        w_u = wl[4 * W + n_gate:]
        w_gates = jnp.pad(wl[4 * W:4 * W + n_gate], ((0, LANES - n_gate), (0, 0))).astype(BF16)
        bias_row = jnp.pad(b_gates[l].astype(F32), (0, LANES - n_gate)).reshape(1, LANES)
        s5w = _s5_weights(s5_lambda_re[l].astype(F32), s5_lambda_im[l].astype(F32), s5_log_dt[l],
                          s5_b_re[l].astype(F32), s5_b_im[l].astype(F32), s5_c_re[l].astype(F32),
                          s5_c_im[l].astype(F32), s5_d[l], s5_glu_w[l].astype(F32))
        wo = w_out[l]

        z, gates = _inproj(h, norm1_g[l].reshape(1, D).astype(F32), wl, w_u, w_gates)
        hm = _mlstm(z, gates, bias_row, conv_qk_w[l].astype(F32),
                    mlstm_norm_g[l].reshape(1, W).astype(F32), B, S)
        ys = _s5(z, *s5w, B, S)
        h1, xn2 = _outproj(hm, ys, wo, h, norm2_g[l].reshape(1, D).astype(F32))

        g3 = _route_gates(xn2, peer_wq[l].astype(BF16), peer_subkeys[l].astype(BF16))
        assert l == depth - 1, "kernel is specialised to a single layer followed by the final norm"
        h = _experts(xn2, peer_u[l], peer_v[l], g3, h1,
                     final_g.reshape(1, D).astype(F32))
    return h.reshape(B, S, D)
```

```python
import functools
import math

import jax
import jax.numpy as jnp
import numpy as np
from jax import lax
from jax.experimental import pallas as pl
from jax.experimental.pallas import tpu as pltpu

F32 = jnp.float32
BF16 = jnp.bfloat16

D_MODEL = 2048
MLSTM_WIDTH = 1024
MLSTM_HEADS = 4
HEAD_DIM = 256
CHUNK = 128
CONV_WIDTH = 4
S5_WIDTH = 1024
S5_GROUP = 16
S5_GROUPS = 64
S5_STATE = 64
PEER_HEADS = 8
PEER_NKEYS = 128
PEER_EXPERTS = PEER_NKEYS * PEER_NKEYS
PEER_TOPK = 16
PEER_QDIM = 256
RMS_EPS = 1e-6

LANES = 128
SUBLANES = 8
VMEM_LIMIT = 56 * 1024 * 1024

NT_DIMS = (((1,), (1,)), ((), ()))


def _params(*sem):
    return pltpu.CompilerParams(dimension_semantics=sem, vmem_limit_bytes=VMEM_LIMIT)


def _rms_scale(v):
    return lax.rsqrt(jnp.mean(v * v, axis=-1, keepdims=True) + RMS_EPS)


def _gelu(v):
    return 0.5 * v * (1.0 + lax.erf(v * np.float32(math.sqrt(0.5))))


def _sigmoid(v):
    return 1.0 / (1.0 + jnp.exp(-v))


def _log_sigmoid(v):
    return jnp.minimum(v, 0.0) - jnp.log1p(jnp.exp(-jnp.abs(v)))


def _inproj_kernel(x_ref, g_ref, w_ref, wu_ref, wg_ref, z_ref, gates_ref, xn_ref):
    j = pl.program_id(1)
    last = pl.num_programs(1) - 1

    @pl.when(j == 0)
    def _():
        x = x_ref[...]
        xn_ref[...] = (x * _rms_scale(x) * g_ref[...]).astype(BF16)
        gates_ref[...] = jnp.dot(xn_ref[...], wg_ref[...], preferred_element_type=F32)

    @pl.when(j < last)
    def _():
        z_ref[...] = jnp.dot(xn_ref[...], w_ref[...], preferred_element_type=F32)

    @pl.when(j == last)
    def _():
        z_ref[...] = jnp.dot(xn_ref[...], wu_ref[...], preferred_element_type=F32)


def _inproj(x2d, g_row, w_all, w_u, w_gates, tm=1024, tn=1024):
    T, D = x2d.shape
    n_main = (4 * MLSTM_WIDTH) // tn
    assert w_u.shape == (D, tn)
    return pl.pallas_call(
        _inproj_kernel,
        grid=(T // tm, n_main + 1),
        in_specs=[
            pl.BlockSpec((tm, D), lambda i, j: (i, 0)),
            pl.BlockSpec((1, D), lambda i, j: (0, 0)),
            pl.BlockSpec((D, tn), lambda i, j: (0, jnp.minimum(j, n_main - 1))),
            pl.BlockSpec((D, tn), lambda i, j: (0, 0)),
            pl.BlockSpec((D, LANES), lambda i, j: (0, 0)),
        ],
        out_specs=[
            pl.BlockSpec((tm, tn), lambda i, j: (i, j)),
            pl.BlockSpec((tm, LANES), lambda i, j: (i, 0)),
        ],
        out_shape=[jax.ShapeDtypeStruct((T, (n_main + 1) * tn), F32), jax.ShapeDtypeStruct((T, LANES), F32)],
        scratch_shapes=[pltpu.VMEM((tm, D), BF16)],
        compiler_params=_params("parallel", "arbitrary"),
        name="inproj",
    )(x2d, g_row, w_all, w_u, w_gates)


def _mlstm_kernel(q_ref, k_ref, v_ref, o_ref, gates_ref, bias_ref, convw_ref, ng_ref, hm_ref,
                  xe_ref, c_ref, n_ref, m_ref):
    L, H, Dh, W = CHUNK, MLSTM_HEADS, HEAD_DIM, MLSTM_WIDTH
    tail = SUBLANES

    @pl.when(pl.program_id(1) == 0)
    def _():
        xe_ref[0:tail, :] = jnp.zeros((tail, 2 * W), F32)
        c_ref[...] = jnp.zeros_like(c_ref)
        n_ref[...] = jnp.zeros_like(n_ref)
        m_ref[...] = jnp.zeros_like(m_ref)

    xe_ref[tail:tail + L, 0:W] = q_ref[...]
    xe_ref[tail:tail + L, W:2 * W] = k_ref[...]
    ext = xe_ref[0:tail + L, :].astype(BF16)
    pick_r = lax.broadcasted_iota(jnp.int32, ((CONV_WIDTH - 1) * L, tail + L), 0)
    pick_c = lax.broadcasted_iota(jnp.int32, ((CONV_WIDTH - 1) * L, tail + L), 1)
    pick = jnp.where(pick_c == (pick_r % L) + (tail - (CONV_WIDTH - 1)) + pick_r // L, 1.0, 0.0).astype(BF16)
    shifted = jnp.dot(pick, ext, preferred_element_type=F32)
    conv = convw_ref[CONV_WIDTH - 1:CONV_WIDTH, :] * xe_ref[tail:tail + L, :]
    for j in range(CONV_WIDTH - 1):
        conv = conv + convw_ref[j:j + 1, :] * shifted[j * L:(j + 1) * L, :]
    xe_ref[0:tail, :] = xe_ref[L:L + tail, :]
    qk = conv * _sigmoid(conv)

    g = gates_ref[...] + bias_ref[...]
    g_t = g.T
    row = lax.broadcasted_iota(jnp.int32, (L, L), 0)
    col = lax.broadcasted_iota(jnp.int32, (L, L), 1)
    causal = col <= row
    tri = jnp.where(causal, 1.0, 0.0).astype(F32)
    tri_t = jnp.where(row <= col, 1.0, 0.0).astype(F32)
    hi = lax.Precision.HIGHEST
    a_cols = jnp.dot(tri, _log_sigmoid(g), precision=hi, preferred_element_type=F32)
    a_rows = jnp.dot(_log_sigmoid(g_t), tri_t, precision=hi, preferred_element_type=F32)

    scale = np.float32(Dh ** -0.5)
    for h in range(H):
        a_c = a_cols[:, H + h:H + h + 1]
        a_r = a_rows[H + h:H + h + 1, :]
        li_c = g[:, h:h + 1]
        li_r = g_t[h:h + 1, :]
        a_tot = a_r[:, L - 1:L]
        m0 = m_ref[h]
        c0 = c_ref[h]
        n0 = n_ref[h]

        q = qk[:, h * Dh:(h + 1) * Dh] * scale
        k = qk[:, W + h * Dh:W + (h + 1) * Dh]
        v = v_ref[:, h * Dh:(h + 1) * Dh]
        qb, kb, vb = q.astype(BF16), k.astype(BF16), v.astype(BF16)

        dmat = jnp.where(causal, a_c - a_r + li_r, -jnp.inf)
        m_inter = a_c + m0
        m_t = jnp.maximum(m_inter, jnp.max(dmat, axis=-1, keepdims=True))
        wm = jnp.exp(dmat - m_t)
        s_inter = jnp.exp(m_inter - m_t)
        s = lax.dot_general(qb, kb, NT_DIMS, preferred_element_type=F32) * wm
        num = (jnp.dot(s.astype(BF16), vb, preferred_element_type=F32)
               + s_inter * jnp.dot(qb, c0.astype(BF16), preferred_element_type=F32))
        den = (jnp.sum(s, axis=-1, keepdims=True)
               + s_inter * jnp.sum(q * n0, axis=-1, keepdims=True))
        hh = num / jnp.maximum(jnp.abs(den), jnp.exp(-m_t))
        hh = hh * _rms_scale(hh)
        og = _sigmoid(o_ref[:, h * Dh:(h + 1) * Dh])
        hm_ref[:, h * Dh:(h + 1) * Dh] = (hh * ng_ref[:, h * Dh:(h + 1) * Dh] * og).astype(BF16)

        g_r = a_tot - a_r + li_r
        g_c = a_tot - a_c + li_c
        m_loc = jnp.max(g_r, axis=-1, keepdims=True)
        kw = k * jnp.exp(g_c - m_loc)
        d_c = jnp.dot(kw.T.astype(BF16), vb, preferred_element_type=F32)
        d_n = jnp.sum(kw, axis=0, keepdims=True)
        m_new = jnp.maximum(a_tot + m0, m_loc)
        s_old = jnp.exp(a_tot + m0 - m_new)
        s_new = jnp.exp(m_loc - m_new)
        c_ref[h] = s_old * c0 + s_new * d_c
        n_ref[h] = s_old * n0 + s_new * d_n
        m_ref[h] = m_new


def _mlstm(z, gates, bias_row, conv_w, norm_g_row, batch, seq):
    T = z.shape[0]
    L, W = CHUNK, MLSTM_WIDTH
    nc = seq // L
    blk = lambda col: pl.BlockSpec((L, W), lambda b, c, col=col: (b * nc + c, col))
    return pl.pallas_call(
        _mlstm_kernel,
        grid=(batch, nc),
        in_specs=[
            blk(0), blk(1), blk(2), blk(3),
            pl.BlockSpec((L, LANES), lambda b, c: (b * nc + c, 0)),
            pl.BlockSpec((1, LANES), lambda b, c: (0, 0)),
            pl.BlockSpec((CONV_WIDTH, 2 * W), lambda b, c: (0, 0)),
            pl.BlockSpec((1, W), lambda b, c: (0, 0)),
        ],
        out_specs=pl.BlockSpec((L, W), lambda b, c: (b * nc + c, 0)),
        out_shape=jax.ShapeDtypeStruct((T, W), BF16),
        scratch_shapes=[
            pltpu.VMEM((L + 2 * SUBLANES, 2 * W), F32),
            pltpu.VMEM((MLSTM_HEADS, HEAD_DIM, HEAD_DIM), F32),
            pltpu.VMEM((MLSTM_HEADS, 1, HEAD_DIM), F32),
            pltpu.VMEM((MLSTM_HEADS, 1, 1), F32),
        ],
        compiler_params=_params("parallel", "arbitrary"),
        name="mlstm",
    )(z, z, z, z, gates, bias_row, conv_w, norm_g_row)


S5_BLOCK = 256
S5_SEG = S5_BLOCK // SUBLANES
S5_TILES = S5_WIDTH // LANES
S5_TILE_STATE = 8 * S5_STATE


def _s5_kernel(u_ref, win_ref, wout_ref, wglu_ref, a_ref, aseg_ref, d_ref, ys_ref,
               up_ref, x_ref, xs_ref, yp_ref, carry_ref):
    nseg, seg, half = SUBLANES, S5_SEG, S5_TILE_STATE
    chains = range(u_ref.shape[0])

    @pl.when(pl.program_id(1) == 0)
    def _():
        carry_ref[...] = jnp.zeros_like(carry_ref)

    def cmul_add(ar, ai, sr, si, xr, xi):
        return ar * sr - ai * si + xr, ar * si + ai * sr + xi

    def x_rows(b, i):
        rows = slice(nseg * i, nseg * (i + 1))
        return x_ref[b, rows, 0:half], x_ref[b, rows, half:2 * half]

    for i in range(seg):
        for b in chains:
            up_ref[b, nseg * i:nseg * (i + 1), :] = u_ref[b, pl.ds(i, nseg, stride=seg), :]
    for b in chains:
        x_ref[b] = jnp.dot(up_ref[b].astype(BF16), win_ref[...], preferred_element_type=F32)
    ar = a_ref[:, 0:half]
    ai = a_ref[:, half:2 * half]

    fin = [(jnp.zeros((nseg, half), F32), jnp.zeros((nseg, half), F32)) for _ in chains]
    for i in range(seg):
        for b in chains:
            fin[b] = cmul_add(ar, ai, *fin[b], *x_rows(b, i))

    lr = aseg_ref[:, 0:half]
    lim = aseg_ref[:, half:2 * half]
    state = []
    for b in chains:
        cr = carry_ref[b, :, 0:half]
        ci = carry_ref[b, :, half:2 * half]
        crs, cis = [], []
        for r in range(nseg):
            crs.append(cr)
            cis.append(ci)
            cr, ci = cmul_add(lr, lim, cr, ci, fin[b][0][r:r + 1, :], fin[b][1][r:r + 1, :])
        carry_ref[b, :, 0:half] = cr
        carry_ref[b, :, half:2 * half] = ci
        state.append((jnp.concatenate(crs, axis=0), jnp.concatenate(cis, axis=0)))

    for i in range(seg):
        rows = slice(nseg * i, nseg * (i + 1))
        for b in chains:
            state[b] = cmul_add(ar, ai, *state[b], *x_rows(b, i))
            xs_ref[b, rows, 0:half] = state[b][0]
            xs_ref[b, rows, half:2 * half] = state[b][1]

    for b in chains:
        y = jnp.dot(xs_ref[b].astype(BF16), wout_ref[...], preferred_element_type=F32)
        y = y + d_ref[...] * up_ref[b]
        ab = jnp.dot(_gelu(y).astype(BF16), wglu_ref[...], preferred_element_type=F32)
        yp_ref[b] = ab[:, 0:LANES] * _sigmoid(ab[:, LANES:2 * LANES])
    for i in range(seg):
        for b in chains:
            ys_ref[b, pl.ds(i, nseg, stride=seg), :] = yp_ref[b, nseg * i:nseg * (i + 1), :]


def _s5(z, win, wout, wglu, a_rows, aseg_rows, d_row, batch, seq):
    T, N = z.shape
    blk = S5_BLOCK
    nb = seq // blk
    u_col = (4 * MLSTM_WIDTH) // LANES
    tile = lambda shape: pl.BlockSpec((None,) + shape, lambda j, c: (j,) + (0,) * len(shape))
    ys = pl.pallas_call(
        _s5_kernel,
        grid=(S5_TILES, nb),
        in_specs=[
            pl.BlockSpec((batch, blk, LANES), lambda j, c: (0, c, u_col + j)),
            tile(win.shape[1:]), tile(wout.shape[1:]), tile(wglu.shape[1:]),
            tile(a_rows.shape[1:]), tile(aseg_rows.shape[1:]),
            pl.BlockSpec((1, LANES), lambda j, c: (0, j)),
        ],
        out_specs=pl.BlockSpec((batch, blk, LANES), lambda j, c: (0, c, j)),
        out_shape=jax.ShapeDtypeStruct((batch, seq, S5_WIDTH), F32),
        scratch_shapes=[
            pltpu.VMEM((batch, blk, LANES), F32),
            pltpu.VMEM((batch, blk, 2 * S5_TILE_STATE), F32),
            pltpu.VMEM((batch, blk, 2 * S5_TILE_STATE), F32),
            pltpu.VMEM((batch, blk, LANES), F32),
            pltpu.VMEM((batch, 1, 2 * S5_TILE_STATE), F32),
        ],
        compiler_params=_params("parallel", "arbitrary"),
        name="s5",
    )(z.reshape(batch, seq, N), win, wout, wglu, a_rows, aseg_rows, d_row)
    return ys.reshape(T, S5_WIDTH)


def _s5_weights(lam_re, lam_im, log_dt, b_re, b_im, c_re, c_im, d, glu_w):
    G, P, Hg = S5_GROUPS, S5_STATE, S5_GROUP
    nt, gl = S5_TILES, G // S5_TILES
    dt = jnp.exp(log_dt.astype(F32))[:, None]
    mag = jnp.exp(lam_re * dt)
    ar = mag * jnp.cos(lam_im * dt)
    ai = mag * jnp.sin(lam_im * dt)
    den = lam_re * lam_re + lam_im * lam_im
    fr = ((ar - 1.0) * lam_re + ai * lam_im) / den
    fi = (ai * lam_re - (ar - 1.0) * lam_im) / den
    bbr = fr[..., None] * b_re - fi[..., None] * b_im
    bbi = fr[..., None] * b_im + fi[..., None] * b_re
    eye = jnp.eye(gl, dtype=F32)

    def tiles(v):
        return v.reshape((nt, gl) + v.shape[1:])

    bb = jnp.stack([tiles(bbr), tiles(bbi)], axis=2)
    win = jnp.einsum('tgrph,gk->tghrkp', bb, eye).reshape(nt, gl * Hg, 2 * gl * P)
    cc = jnp.stack([tiles(c_re), -tiles(c_im)], axis=2)
    wout = jnp.einsum('tgrhp,gk->trkpgh', cc, eye).reshape(nt, 2 * gl * P, gl * Hg)
    gw = tiles(glu_w).reshape(nt, gl, Hg, 2, Hg)
    wglu = jnp.einsum('tghrk,gq->tqhrgk', gw, eye).reshape(nt, gl * Hg, 2 * gl * Hg)

    def rows(re, im):
        return jnp.concatenate([tiles(re).reshape(nt, 1, gl * P), tiles(im).reshape(nt, 1, gl * P)], axis=-1)

    a_rows = jnp.broadcast_to(rows(ar, ai), (nt, SUBLANES, 2 * gl * P))
    pr, pi = ar, ai
    for _ in range(int(math.log2(S5_SEG))):
        pr, pi = pr * pr - pi * pi, 2.0 * pr * pi
    aseg_rows = rows(pr, pi)
    d_row = d.reshape(1, G * Hg).astype(F32)
    return win.astype(BF16), wout.astype(BF16), wglu.astype(BF16), a_rows, aseg_rows, d_row


def _outproj_kernel(hm_ref, ys_ref, w1_ref, w2_ref, x_ref, g_ref, h1_ref, xn_ref):
    mix = (jnp.dot(hm_ref[...], w1_ref[...].astype(BF16), preferred_element_type=F32)
           + jnp.dot(ys_ref[...].astype(BF16), w2_ref[...].astype(BF16), preferred_element_type=F32))
    h1 = x_ref[...] + mix
    h1_ref[...] = h1
    xn_ref[...] = (h1 * _rms_scale(h1) * g_ref[...]).astype(BF16)


def _outproj(hm, ys, w, x2d, g_row, tm=512):
    T, D = x2d.shape
    W = hm.shape[1]
    row = lambda w: pl.BlockSpec((tm, w), lambda i: (i, 0))
    half = lambda k: pl.BlockSpec((W, D), lambda i: (k, 0), pipeline_mode=pl.Buffered(1))
    return pl.pallas_call(
        _outproj_kernel,
        grid=(T // tm,),
        in_specs=[row(W), row(W), half(0), half(1), row(D), pl.BlockSpec((1, D), lambda i: (0, 0))],
        out_specs=[row(D), row(D)],
        out_shape=[jax.ShapeDtypeStruct((T, D), F32), jax.ShapeDtypeStruct((T, D), BF16)],
        compiler_params=_params("parallel"),
        name="outproj",
    )(hm, ys, w, w, x2d, g_row)


ROUTE_GROUP = 1024
ROUTE_CHUNK = LANES
GATE_PIECE = 4
GATE_LAG = 1
assert PEER_HEADS == SUBLANES


def _sort_pairs(n):
    pairs = []
    p = 1
    while p < n:
        k = p
        while k >= 1:
            for j in range(k % p, n - k, 2 * k):
                for i in range(min(k, n - j - k)):
                    if (i + j) // (2 * p) == (i + j + k) // (2 * p):
                        pairs.append((i + j, i + j + k))
            k //= 2
        p *= 2
    return pairs


def _first_second(a, b, ties=True):
    (av, ai), (bv, bi) = a, b
    if ties:
        tie = av == bv
        a_first = jnp.where(tie, bi, av) > jnp.where(tie, ai, bv)
    else:
        a_first = av > bv
    return ((jnp.maximum(av, bv), jnp.where(a_first, ai, bi)),
            (jnp.minimum(av, bv), jnp.where(a_first, bi, ai)))


def _sorted_group(items, ties=True):
    items = list(items)
    for i, j in _sort_pairs(len(items)):
        items[i], items[j] = _first_second(items[i], items[j], ties)
    return items


def _merge_top(a, b, ties=True):
    n = len(a)
    c = [_first_second(a[v], b[n - 1 - v], ties)[0] for v in range(n)]
    j = n // 2
    while j >= 1:
        for i in range(n):
            if (i & j) == 0:
                c[i], c[i + j] = _first_second(c[i], c[i + j], ties)
        j //= 2
    return c


def _top_sorted(groups, stage=lambda items: items, ties=True):
    groups = list(groups)
    while len(groups) > 1:
        groups = [_merge_top(stage(groups[i]), groups[i + 1], ties) for i in range(0, len(groups), 2)]
    return groups[0]


def _order_doubt(top, values):
    doubt = functools.reduce(lambda x, y: x + y,
                             [jnp.where(top[r][0] > top[r + 1][0], 0.0, 1.0) for r in range(len(top) - 1)])
    last = top[-1][0]
    count = functools.reduce(lambda x, y: x + y, [jnp.where(v >= last, 1.0, 0.0) for v in values])
    return doubt + jnp.where(count == float(len(top)), 0.0, 1.0)


def _product_top(top1, top2, stage=lambda items: items, ties=True):
    K = PEER_TOPK
    pairs = [(r1, r2) for r1 in range(K) for r2 in range(K) if (r1 + 1) * (r2 + 1) <= K]
    cands = [(top1[r1][0] + top2[r2][0], jnp.full_like(top1[0][0], float(r1 * K + r2))) for r1, r2 in pairs]
    pad = (jnp.full_like(top1[0][0], -jnp.inf), jnp.full_like(top1[0][0], float(K * K)))
    cands += [pad] * (-len(cands) % K)
    groups = [cands[:K]]
    for g in range(K, len(cands), K):
        groups.append(_sorted_group(stage(cands[g:g + K]), ties))
    best = _top_sorted(groups, stage, ties)
    doubt = None if ties else _order_doubt(best, [v for v, _ in cands[:len(pairs)]])
    top_s = [v for v, _ in best]
    e = [jnp.exp(v - top_s[0]) for v in top_s]
    z = functools.reduce(lambda x, y: x + y, e)
    gates, i1s, i2s = [], [], []
    for r in range(K):
        pos = best[r][1]
        r1 = jnp.floor(pos * np.float32(1.0 / K))
        r2 = pos - r1 * np.float32(K)
        i1, i2 = top1[0][1], top2[0][1]
        for a in range(1, K):
            i1 = jnp.where(r1 == float(a), top1[a][1], i1)
            i2 = jnp.where(r2 == float(a), top2[a][1], i2)
        gates.append(e[r] / z)
        i1s.append(i1)
        i2s.append(i2)
    return i1s, i2s, gates, doubt


def _route_gates_kernel(x_ref, wq_ref, sk_ref, g_ref, q_ref, sc_ref, slot_ref, tok_ref):
    K, n, half, H = PEER_TOPK, PEER_NKEYS, PEER_QDIM // 2, PEER_HEADS
    s = pl.program_id(0)
    n_chunks = pl.num_programs(0) - 1
    per_group = ROUTE_GROUP // ROUTE_CHUNK
    cur = s % 2

    @pl.when(s == 0)
    def _():
        slot_ref[...] = jnp.zeros_like(slot_ref)

    @pl.when(jnp.logical_and(s % per_group == 0, s < n_chunks))
    def _():
        q = jnp.dot(x_ref[...], wq_ref[...], preferred_element_type=F32)
        for hh in range(H):
            q_ref[hh] = q[:, hh * PEER_QDIM:(hh + 1) * PEER_QDIM].astype(BF16)

    for a in range(3):
        tok_ref[a] = slot_ref[1 - cur, a].T
    keys = lax.broadcasted_iota(jnp.int32, (n, n), 0).astype(F32)
    pieces = iter(range(ROUTE_CHUNK // GATE_PIECE))
    done = []

    def gate_piece():
        k = next(pieces, None)
        if k is None:
            return
        for t in range(k * GATE_PIECE, (k + 1) * GATE_PIECE):
            at = jnp.where(keys == tok_ref[0, t:t + 1, :], 1.0, 0.0).astype(BF16)
            bt = jnp.where(keys == tok_ref[1, t:t + 1, :], tok_ref[2, t:t + 1, :], 0.0).astype(BF16)
            g = lax.dot_general(at, bt, NT_DIMS, preferred_element_type=F32)
            g_ref[t] = g
        done.append(g[0:SUBLANES, :] * 0.0)

    def stage(items):
        gate_piece()
        if len(done) > GATE_LAG:
            (v, i), rest = items[0], list(items[1:])
            return [(v + done[len(done) - 1 - GATE_LAG], i)] + rest
        return items

    row0 = pl.multiple_of((s % per_group) * ROUTE_CHUNK, ROUTE_CHUNK)
    for p in range(2):
        for h in range(H):
            qc = q_ref[h, pl.ds(row0, ROUTE_CHUNK), p * half:(p + 1) * half]
            sc_ref[p, pl.ds(h, n, stride=H), :] = lax.dot_general(
                sk_ref[p], qc, NT_DIMS, preferred_element_type=F32)

    def route(ties, stage):
        tops, doubt = [], None
        for p in range(2):
            items = [(sc_ref[p, k * SUBLANES:(k + 1) * SUBLANES, :], jnp.full((SUBLANES, LANES), float(k), F32))
                     for k in range(n)]
            groups = [_sorted_group(stage(items[g:g + K]), ties) for g in range(0, n, K)]
            tops.append(_top_sorted(groups, stage, ties))
            if not ties:
                d = _order_doubt(tops[-1], [v for v, _ in items])
                doubt = d if doubt is None else doubt + d
        i1s, i2s, gates, d = _product_top(*tops, stage, ties)
        return (i1s, i2s, gates), (None if ties else doubt + d)

    def store(slots):
        for a, vals in enumerate(slots):
            for r in range(K):
                slot_ref[cur, a, pl.ds(r, H, stride=K), :] = vals[r]

    slots, doubt = route(False, stage)
    for _ in range(ROUTE_CHUNK // GATE_PIECE):
        gate_piece()
    store(slots)

    @pl.when(jnp.max(doubt) > 0.0)
    def _():
        store(route(True, lambda items: items)[0])


def _route_gates(xn2, wq, subkeys):
    T, D = xn2.shape
    n, slots = PEER_NKEYS, PEER_HEADS * PEER_TOPK
    n_chunks = T // ROUTE_CHUNK
    per_group = ROUTE_GROUP // ROUTE_CHUNK
    n_groups = T // ROUTE_GROUP
    once = pl.Buffered(1)
    return pl.pallas_call(
        _route_gates_kernel,
        grid=(n_chunks + 1,),
        in_specs=[
            pl.BlockSpec((ROUTE_GROUP, D), lambda s: (jnp.minimum(s // per_group, n_groups - 1), 0)),
            pl.BlockSpec(wq.shape, lambda s: (0, 0), pipeline_mode=once),
            pl.BlockSpec(subkeys.shape, lambda s: (0, 0, 0)),
        ],
        out_specs=pl.BlockSpec((ROUTE_CHUNK, n, n), lambda s: (jnp.maximum(s - 1, 0), 0, 0)),
        out_shape=jax.ShapeDtypeStruct((T, n, n), F32),
        scratch_shapes=[pltpu.VMEM((PEER_HEADS, ROUTE_GROUP, PEER_QDIM), BF16),
                        pltpu.VMEM((2, n * SUBLANES, LANES), F32),
                        pltpu.VMEM((2, 3, slots, ROUTE_CHUNK), F32),
                        pltpu.VMEM((3, ROUTE_CHUNK, slots), F32)],
        compiler_params=_params("arbitrary"),
        name="peer_route_gates",
    )(xn2, wq, subkeys)


EXPERT_G_ROWS = SUBLANES


def _experts_kernel(x_ref, u_ref, v_ref, g_ref, h1_ref, fg_ref, y_ref):
    e, k = pl.program_id(1), pl.program_id(2)
    first = jnp.logical_and(e == 0, k == 0)
    last = jnp.logical_and(e == pl.num_programs(1) - 1, k == pl.num_programs(2) - 1)

    @pl.when(first)
    def _():
        y_ref[...] = jnp.zeros_like(y_ref)

    u = u_ref[...].astype(BF16)
    v = v_ref[...].astype(BF16)
    act = _gelu(lax.dot_general(x_ref[...], u, NT_DIMS, preferred_element_type=F32))
    n = PEER_NKEYS
    tm, rows, _ = g_ref.shape
    nc = u_ref.shape[0] // n
    g_rows = g_ref.reshape(tm * rows, n)
    w = jnp.concatenate([(g_rows[pl.ds(k * nc + c, tm, stride=rows), :] * act[:, c * n:(c + 1) * n]).astype(BF16)
                         for c in range(nc)], axis=1)
    y_ref[...] += jnp.dot(w, v, preferred_element_type=F32)

    @pl.when(last)
    def _():
        h2 = h1_ref[...] + y_ref[...]
        y_ref[...] = h2 * _rms_scale(h2) * fg_ref[...]


def _experts(xn2, u_tab, v_tab, g3, h1, fg_row, tm=1024, te=512):
    T, D = xn2.shape
    E = u_tab.shape[0]
    n = PEER_NKEYS
    sub = EXPERT_G_ROWS * n // te
    once = pl.Buffered(1)
    return pl.pallas_call(
        _experts_kernel,
        grid=(T // tm, E // (te * sub), sub),
        in_specs=[
            pl.BlockSpec((tm, D), lambda i, e, k: (i, 0), pipeline_mode=once),
            pl.BlockSpec((te, D), lambda i, e, k: (e * sub + k, 0)),
            pl.BlockSpec((te, D), lambda i, e, k: (e * sub + k, 0)),
            pl.BlockSpec((tm, EXPERT_G_ROWS, n), lambda i, e, k: (i, e, 0)),
            pl.BlockSpec((tm, D), lambda i, e, k: (i, 0), pipeline_mode=once),
            pl.BlockSpec((1, D), lambda i, e, k: (0, 0)),
        ],
        out_specs=pl.BlockSpec((tm, D), lambda i, e, k: (i, 0), pipeline_mode=once),
        out_shape=jax.ShapeDtypeStruct((T, D), F32),
        compiler_params=_params("parallel", "arbitrary", "arbitrary"),
        name="peer_experts",
    )(xn2, u_tab, v_tab, g3, h1, fg_row)


def kernel(x, norm1_g, w_in, b_gates, conv_qk_w, mlstm_norm_g, s5_lambda_re, s5_lambda_im, s5_log_dt,
           s5_b_re, s5_b_im, s5_c_re, s5_c_im, s5_d, s5_glu_w, w_out, norm2_g, peer_wq, peer_subkeys,
           peer_u, peer_v, final_g):
    B, S, D = x.shape
    W, H = MLSTM_WIDTH, MLSTM_HEADS
    depth = norm1_g.shape[0]
    h = x.reshape(B * S, D)
    for l in range(depth):
        wl = w_in[l].astype(BF16)
        n_gate = 2 * H
        w_u = wl[:, 4 * W + n_gate:]
        w_gates = jnp.pad(wl[:, 4 * W:4 * W + n_gate], ((0, 0), (0, LANES - n_gate)))
        bias_row = jnp.pad(b_gates[l].astype(F32), (0, LANES - n_gate)).reshape(1, LANES)
        s5w = _s5_weights(s5_lambda_re[l].astype(F32), s5_lambda_im[l].astype(F32), s5_log_dt[l],
                          s5_b_re[l].astype(F32), s5_b_im[l].astype(F32), s5_c_re[l].astype(F32),
                          s5_c_im[l].astype(F32), s5_d[l], s5_glu_w[l].astype(F32))
        wo = w_out[l]

        z, gates = _inproj(h, norm1_g[l].reshape(1, D).astype(F32), wl, w_u, w_gates)
        hm = _mlstm(z, gates, bias_row, conv_qk_w[l].astype(F32),
                    mlstm_norm_g[l].reshape(1, W).astype(F32), B, S)
        ys = _s5(z, *s5w, B, S)
        h1, xn2 = _outproj(hm, ys, wo, h, norm2_g[l].reshape(1, D).astype(F32))

        g3 = _route_gates(xn2, peer_wq[l].astype(BF16), peer_subkeys[l].astype(BF16))
        assert l == depth - 1, "kernel is specialised to a single layer followed by the final norm"
        h = _experts(xn2, peer_u[l], peer_v[l], g3, h1,
                     final_g.reshape(1, D).astype(F32))
    return h.reshape(B, S, D)
```

```python
import functools
import math

import jax
import jax.numpy as jnp
import numpy as np
from jax import lax
from jax.experimental import pallas as pl
from jax.experimental.pallas import tpu as pltpu

F32 = jnp.float32
BF16 = jnp.bfloat16

D_MODEL = 2048
MLSTM_WIDTH = 1024
MLSTM_HEADS = 4
HEAD_DIM = 256
CHUNK = 128
CONV_WIDTH = 4
S5_WIDTH = 1024
S5_GROUP = 16
S5_GROUPS = 64
S5_STATE = 64
PEER_HEADS = 8
PEER_NKEYS = 128
PEER_EXPERTS = PEER_NKEYS * PEER_NKEYS
PEER_TOPK = 16
PEER_QDIM = 256
RMS_EPS = 1e-6

LANES = 128
SUBLANES = 8
VMEM_LIMIT = 56 * 1024 * 1024

NT_DIMS = (((1,), (1,)), ((), ()))


def _params(*sem):
    return pltpu.CompilerParams(dimension_semantics=sem, vmem_limit_bytes=VMEM_LIMIT)


def _rms_scale(v):
    return lax.rsqrt(jnp.mean(v * v, axis=-1, keepdims=True) + RMS_EPS)


def _gelu(v):
    return 0.5 * v * (1.0 + lax.erf(v * np.float32(math.sqrt(0.5))))


def _sigmoid(v):
    return 1.0 / (1.0 + jnp.exp(-v))


def _log_sigmoid(v):
    return jnp.minimum(v, 0.0) - jnp.log1p(jnp.exp(-jnp.abs(v)))


def _inproj_kernel(x_ref, g_ref, w_ref, wu_ref, wg_ref, z_ref, gates_ref, xn_ref):
    j = pl.program_id(1)
    last = pl.num_programs(1) - 1

    @pl.when(j == 0)
    def _():
        x = x_ref[...]
        xn_ref[...] = (x * _rms_scale(x) * g_ref[...]).astype(BF16)
        gates_ref[...] = jnp.dot(xn_ref[...], wg_ref[...], preferred_element_type=F32)

    @pl.when(j < last)
    def _():
        z_ref[...] = jnp.dot(xn_ref[...], w_ref[...], preferred_element_type=F32)

    @pl.when(j == last)
    def _():
        z_ref[...] = jnp.dot(xn_ref[...], wu_ref[...], preferred_element_type=F32)


def _inproj(x2d, g_row, w_all, w_u, w_gates, tm=1024, tn=1024):
    T, D = x2d.shape
    n_main = (4 * MLSTM_WIDTH) // tn
    assert w_u.shape == (D, tn)
    return pl.pallas_call(
        _inproj_kernel,
        grid=(T // tm, n_main + 1),
        in_specs=[
            pl.BlockSpec((tm, D), lambda i, j: (i, 0)),
            pl.BlockSpec((1, D), lambda i, j: (0, 0)),
            pl.BlockSpec((D, tn), lambda i, j: (0, jnp.minimum(j, n_main - 1))),
            pl.BlockSpec((D, tn), lambda i, j: (0, 0)),
            pl.BlockSpec((D, LANES), lambda i, j: (0, 0)),
        ],
        out_specs=[
            pl.BlockSpec((tm, tn), lambda i, j: (i, j)),
            pl.BlockSpec((tm, LANES), lambda i, j: (i, 0)),
        ],
        out_shape=[jax.ShapeDtypeStruct((T, (n_main + 1) * tn), F32), jax.ShapeDtypeStruct((T, LANES), F32)],
        scratch_shapes=[pltpu.VMEM((tm, D), BF16)],
        compiler_params=_params("parallel", "arbitrary"),
        name="inproj",
    )(x2d, g_row, w_all, w_u, w_gates)


def _mlstm_kernel(q_ref, k_ref, v_ref, o_ref, gates_ref, bias_ref, convw_ref, ng_ref, hm_ref,
                  xe_ref, c_ref, n_ref, m_ref):
    L, H, Dh, W = CHUNK, MLSTM_HEADS, HEAD_DIM, MLSTM_WIDTH
    tail = SUBLANES

    @pl.when(pl.program_id(1) == 0)
    def _():
        xe_ref[0:tail, :] = jnp.zeros((tail, 2 * W), F32)
        c_ref[...] = jnp.zeros_like(c_ref)
        n_ref[...] = jnp.zeros_like(n_ref)
        m_ref[...] = jnp.zeros_like(m_ref)

    xe_ref[tail:tail + L, 0:W] = q_ref[...]
    xe_ref[tail:tail + L, W:2 * W] = k_ref[...]
    ext = xe_ref[0:tail + L, :].astype(BF16)
    pick_r = lax.broadcasted_iota(jnp.int32, ((CONV_WIDTH - 1) * L, tail + L), 0)
    pick_c = lax.broadcasted_iota(jnp.int32, ((CONV_WIDTH - 1) * L, tail + L), 1)
    pick = jnp.where(pick_c == (pick_r % L) + (tail - (CONV_WIDTH - 1)) + pick_r // L, 1.0, 0.0).astype(BF16)
    shifted = jnp.dot(pick, ext, preferred_element_type=F32)
    conv = convw_ref[CONV_WIDTH - 1:CONV_WIDTH, :] * xe_ref[tail:tail + L, :]
    for j in range(CONV_WIDTH - 1):
        conv = conv + convw_ref[j:j + 1, :] * shifted[j * L:(j + 1) * L, :]
    xe_ref[0:tail, :] = xe_ref[L:L + tail, :]
    qk = conv * _sigmoid(conv)

    g = gates_ref[...] + bias_ref[...]
    g_t = g.T
    row = lax.broadcasted_iota(jnp.int32, (L, L), 0)
    col = lax.broadcasted_iota(jnp.int32, (L, L), 1)
    causal = col <= row
    tri = jnp.where(causal, 1.0, 0.0).astype(F32)
    tri_t = jnp.where(row <= col, 1.0, 0.0).astype(F32)
    def split(v):
        hi = v.astype(BF16)
        return hi, (v - hi.astype(F32)).astype(BF16)

    tri_b, tri_tb = tri.astype(BF16), tri_t.astype(BF16)
    c_hi, c_lo = split(_log_sigmoid(g))
    a_cols = (jnp.dot(tri_b, c_hi, preferred_element_type=F32)
              + jnp.dot(tri_b, c_lo, preferred_element_type=F32))
    r_hi, r_lo = split(_log_sigmoid(g_t))
    a_rows = (jnp.dot(r_hi, tri_tb, preferred_element_type=F32)
              + jnp.dot(r_lo, tri_tb, preferred_element_type=F32))

    scale = np.float32(Dh ** -0.5)
    for h in range(H):
        a_c = a_cols[:, H + h:H + h + 1]
        a_r = a_rows[H + h:H + h + 1, :]
        li_c = g[:, h:h + 1]
        li_r = g_t[h:h + 1, :]
        a_tot = a_r[:, L - 1:L]
        m0 = m_ref[h]
        c0 = c_ref[h]
        n0 = n_ref[h]

        q = qk[:, h * Dh:(h + 1) * Dh] * scale
        k = qk[:, W + h * Dh:W + (h + 1) * Dh]
        v = v_ref[:, h * Dh:(h + 1) * Dh]
        qb, kb, vb = q.astype(BF16), k.astype(BF16), v.astype(BF16)

        dmat = jnp.where(causal, a_c - a_r + li_r, -jnp.inf)
        m_inter = a_c + m0
        m_t = jnp.maximum(m_inter, jnp.max(dmat, axis=-1, keepdims=True))
        wm = jnp.exp(dmat - m_t)
        s_inter = jnp.exp(m_inter - m_t)
        s = lax.dot_general(qb, kb, NT_DIMS, preferred_element_type=F32) * wm
        num = (jnp.dot(s.astype(BF16), vb, preferred_element_type=F32)
               + s_inter * jnp.dot(qb, c0.astype(BF16), preferred_element_type=F32))
        den = (jnp.sum(s, axis=-1, keepdims=True)
               + s_inter * jnp.sum(q * n0, axis=-1, keepdims=True))
        hh = num / jnp.maximum(jnp.abs(den), jnp.exp(-m_t))
        hh = hh * _rms_scale(hh)
        og = _sigmoid(o_ref[:, h * Dh:(h + 1) * Dh])
        hm_ref[:, h * Dh:(h + 1) * Dh] = (hh * ng_ref[:, h * Dh:(h + 1) * Dh] * og).astype(BF16)

        g_r = a_tot - a_r + li_r
        g_c = a_tot - a_c + li_c
        m_loc = jnp.max(g_r, axis=-1, keepdims=True)
        kw = k * jnp.exp(g_c - m_loc)
        d_c = jnp.dot(kw.T.astype(BF16), vb, preferred_element_type=F32)
        d_n = jnp.sum(kw, axis=0, keepdims=True)
        m_new = jnp.maximum(a_tot + m0, m_loc)
        s_old = jnp.exp(a_tot + m0 - m_new)
        s_new = jnp.exp(m_loc - m_new)
        c_ref[h] = s_old * c0 + s_new * d_c
        n_ref[h] = s_old * n0 + s_new * d_n
        m_ref[h] = m_new


def _mlstm(z, gates, bias_row, conv_w, norm_g_row, batch, seq):
    T = z.shape[0]
    L, W = CHUNK, MLSTM_WIDTH
    nc = seq // L
    blk = lambda col: pl.BlockSpec((L, W), lambda b, c, col=col: (b * nc + c, col))
    return pl.pallas_call(
        _mlstm_kernel,
        grid=(batch, nc),
        in_specs=[
            blk(0), blk(1), blk(2), blk(3),
            pl.BlockSpec((L, LANES), lambda b, c: (b * nc + c, 0)),
            pl.BlockSpec((1, LANES), lambda b, c: (0, 0)),
            pl.BlockSpec((CONV_WIDTH, 2 * W), lambda b, c: (0, 0)),
            pl.BlockSpec((1, W), lambda b, c: (0, 0)),
        ],
        out_specs=pl.BlockSpec((L, W), lambda b, c: (b * nc + c, 0)),
        out_shape=jax.ShapeDtypeStruct((T, W), BF16),
        scratch_shapes=[
            pltpu.VMEM((L + 2 * SUBLANES, 2 * W), F32),
            pltpu.VMEM((MLSTM_HEADS, HEAD_DIM, HEAD_DIM), F32),
            pltpu.VMEM((MLSTM_HEADS, 1, HEAD_DIM), F32),
            pltpu.VMEM((MLSTM_HEADS, 1, 1), F32),
        ],
        compiler_params=_params("parallel", "arbitrary"),
        name="mlstm",
    )(z, z, z, z, gates, bias_row, conv_w, norm_g_row)


S5_BLOCK = 256
S5_SEG = S5_BLOCK // SUBLANES
S5_TILES = S5_WIDTH // LANES
S5_TILE_STATE = 8 * S5_STATE


def _s5_kernel(u_ref, win_ref, wout_ref, wglu_ref, a_ref, aseg_ref, d_ref, ys_ref,
               up_ref, x_ref, xs_ref, yp_ref, carry_ref):
    nseg, seg, half = SUBLANES, S5_SEG, S5_TILE_STATE
    chains = range(u_ref.shape[0])

    @pl.when(pl.program_id(1) == 0)
    def _():
        carry_ref[...] = jnp.zeros_like(carry_ref)

    def cmul_add(ar, ai, sr, si, xr, xi):
        return ar * sr - ai * si + xr, ar * si + ai * sr + xi

    def x_rows(b, i):
        rows = slice(nseg * i, nseg * (i + 1))
        return x_ref[b, rows, 0:half], x_ref[b, rows, half:2 * half]

    for i in range(seg):
        for b in chains:
            up_ref[b, nseg * i:nseg * (i + 1), :] = u_ref[b, pl.ds(i, nseg, stride=seg), :]
    for b in chains:
        x_ref[b] = jnp.dot(up_ref[b].astype(BF16), win_ref[...], preferred_element_type=F32)
    ar = a_ref[:, 0:half]
    ai = a_ref[:, half:2 * half]

    fin = [(jnp.zeros((nseg, half), F32), jnp.zeros((nseg, half), F32)) for _ in chains]
    for i in range(seg):
        for b in chains:
            fin[b] = cmul_add(ar, ai, *fin[b], *x_rows(b, i))

    lr = aseg_ref[:, 0:half]
    lim = aseg_ref[:, half:2 * half]
    state = []
    for b in chains:
        cr = carry_ref[b, :, 0:half]
        ci = carry_ref[b, :, half:2 * half]
        crs, cis = [], []
        for r in range(nseg):
            crs.append(cr)
            cis.append(ci)
            cr, ci = cmul_add(lr, lim, cr, ci, fin[b][0][r:r + 1, :], fin[b][1][r:r + 1, :])
        carry_ref[b, :, 0:half] = cr
        carry_ref[b, :, half:2 * half] = ci
        state.append((jnp.concatenate(crs, axis=0), jnp.concatenate(cis, axis=0)))

    for i in range(seg):
        rows = slice(nseg * i, nseg * (i + 1))
        for b in chains:
            state[b] = cmul_add(ar, ai, *state[b], *x_rows(b, i))
            xs_ref[b, rows, 0:half] = state[b][0]
            xs_ref[b, rows, half:2 * half] = state[b][1]

    for b in chains:
        y = jnp.dot(xs_ref[b].astype(BF16), wout_ref[...], preferred_element_type=F32)
        y = y + d_ref[...] * up_ref[b]
        ab = jnp.dot(_gelu(y).astype(BF16), wglu_ref[...], preferred_element_type=F32)
        yp_ref[b] = ab[:, 0:LANES] * _sigmoid(ab[:, LANES:2 * LANES])
    for i in range(seg):
        for b in chains:
            ys_ref[b, pl.ds(i, nseg, stride=seg), :] = yp_ref[b, nseg * i:nseg * (i + 1), :]


def _s5(z, win, wout, wglu, a_rows, aseg_rows, d_row, batch, seq):
    T, N = z.shape
    blk = S5_BLOCK
    nb = seq // blk
    u_col = (4 * MLSTM_WIDTH) // LANES
    tile = lambda shape: pl.BlockSpec((None,) + shape, lambda j, c: (j,) + (0,) * len(shape))
    ys = pl.pallas_call(
        _s5_kernel,
        grid=(S5_TILES, nb),
        in_specs=[
            pl.BlockSpec((batch, blk, LANES), lambda j, c: (0, c, u_col + j)),
            tile(win.shape[1:]), tile(wout.shape[1:]), tile(wglu.shape[1:]),
            tile(a_rows.shape[1:]), tile(aseg_rows.shape[1:]),
            pl.BlockSpec((1, LANES), lambda j, c: (0, j)),
        ],
        out_specs=pl.BlockSpec((batch, blk, LANES), lambda j, c: (0, c, j)),
        out_shape=jax.ShapeDtypeStruct((batch, seq, S5_WIDTH), F32),
        scratch_shapes=[
            pltpu.VMEM((batch, blk, LANES), F32),
            pltpu.VMEM((batch, blk, 2 * S5_TILE_STATE), F32),
            pltpu.VMEM((batch, blk, 2 * S5_TILE_STATE), F32),
            pltpu.VMEM((batch, blk, LANES), F32),
            pltpu.VMEM((batch, 1, 2 * S5_TILE_STATE), F32),
        ],
        compiler_params=_params("parallel", "arbitrary"),
        name="s5",
    )(z.reshape(batch, seq, N), win, wout, wglu, a_rows, aseg_rows, d_row)
    return ys.reshape(T, S5_WIDTH)


def _s5_weights(lam_re, lam_im, log_dt, b_re, b_im, c_re, c_im, d, glu_w):
    G, P, Hg = S5_GROUPS, S5_STATE, S5_GROUP
    nt, gl = S5_TILES, G // S5_TILES
    dt = jnp.exp(log_dt.astype(F32))[:, None]
    mag = jnp.exp(lam_re * dt)
    ar = mag * jnp.cos(lam_im * dt)
    ai = mag * jnp.sin(lam_im * dt)
    den = lam_re * lam_re + lam_im * lam_im
    fr = ((ar - 1.0) * lam_re + ai * lam_im) / den
    fi = (ai * lam_re - (ar - 1.0) * lam_im) / den
    bbr = fr[..., None] * b_re - fi[..., None] * b_im
    bbi = fr[..., None] * b_im + fi[..., None] * b_re
    eye = jnp.eye(gl, dtype=F32)

    def tiles(v):
        return v.reshape((nt, gl) + v.shape[1:])

    bb = jnp.stack([tiles(bbr), tiles(bbi)], axis=2)
    win = jnp.einsum('tgrph,gk->tghrkp', bb, eye).reshape(nt, gl * Hg, 2 * gl * P)
    cc = jnp.stack([tiles(c_re), -tiles(c_im)], axis=2)
    wout = jnp.einsum('tgrhp,gk->trkpgh', cc, eye).reshape(nt, 2 * gl * P, gl * Hg)
    gw = tiles(glu_w).reshape(nt, gl, Hg, 2, Hg)
    wglu = jnp.einsum('tghrk,gq->tqhrgk', gw, eye).reshape(nt, gl * Hg, 2 * gl * Hg)

    def rows(re, im):
        return jnp.concatenate([tiles(re).reshape(nt, 1, gl * P), tiles(im).reshape(nt, 1, gl * P)], axis=-1)

    a_rows = jnp.broadcast_to(rows(ar, ai), (nt, SUBLANES, 2 * gl * P))
    pr, pi = ar, ai
    for _ in range(int(math.log2(S5_SEG))):
        pr, pi = pr * pr - pi * pi, 2.0 * pr * pi
    aseg_rows = rows(pr, pi)
    d_row = d.reshape(1, G * Hg).astype(F32)
    return win.astype(BF16), wout.astype(BF16), wglu.astype(BF16), a_rows, aseg_rows, d_row


def _outproj_kernel(hm_ref, ys_ref, w1_ref, w2_ref, x_ref, g_ref, h1_ref, xn_ref):
    mix = (jnp.dot(hm_ref[...], w1_ref[...].astype(BF16), preferred_element_type=F32)
           + jnp.dot(ys_ref[...].astype(BF16), w2_ref[...].astype(BF16), preferred_element_type=F32))
    h1 = x_ref[...] + mix
    h1_ref[...] = h1
    xn_ref[...] = (h1 * _rms_scale(h1) * g_ref[...]).astype(BF16)


def _outproj(hm, ys, w, x2d, g_row, tm=512):
    T, D = x2d.shape
    W = hm.shape[1]
    row = lambda w: pl.BlockSpec((tm, w), lambda i: (i, 0))
    half = lambda k: pl.BlockSpec((W, D), lambda i: (k, 0), pipeline_mode=pl.Buffered(1))
    return pl.pallas_call(
        _outproj_kernel,
        grid=(T // tm,),
        in_specs=[row(W), row(W), half(0), half(1), row(D), pl.BlockSpec((1, D), lambda i: (0, 0))],
        out_specs=[row(D), row(D)],
        out_shape=[jax.ShapeDtypeStruct((T, D), F32), jax.ShapeDtypeStruct((T, D), BF16)],
        compiler_params=_params("parallel"),
        name="outproj",
    )(hm, ys, w, w, x2d, g_row)


ROUTE_GROUP = 1024
ROUTE_CHUNK = LANES
GATE_PIECE = 4
GATE_LAG = 1
assert PEER_HEADS == SUBLANES


def _sort_pairs(n):
    pairs = []
    p = 1
    while p < n:
        k = p
        while k >= 1:
            for j in range(k % p, n - k, 2 * k):
                for i in range(min(k, n - j - k)):
                    if (i + j) // (2 * p) == (i + j + k) // (2 * p):
                        pairs.append((i + j, i + j + k))
            k //= 2
        p *= 2
    return pairs


def _first_second(a, b, ties=True):
    (av, ai), (bv, bi) = a, b
    if ties:
        tie = av == bv
        a_first = jnp.where(tie, bi, av) > jnp.where(tie, ai, bv)
    else:
        a_first = av > bv
    return ((jnp.maximum(av, bv), jnp.where(a_first, ai, bi)),
            (jnp.minimum(av, bv), jnp.where(a_first, bi, ai)))


def _sorted_group(items, ties=True):
    items = list(items)
    for i, j in _sort_pairs(len(items)):
        items[i], items[j] = _first_second(items[i], items[j], ties)
    return items


def _merge_top(a, b, ties=True):
    n = len(a)
    c = [_first_second(a[v], b[n - 1 - v], ties)[0] for v in range(n)]
    j = n // 2
    while j >= 1:
        for i in range(n):
            if (i & j) == 0:
                c[i], c[i + j] = _first_second(c[i], c[i + j], ties)
        j //= 2
    return c


def _top_sorted(groups, stage=lambda items: items, ties=True):
    groups = list(groups)
    while len(groups) > 1:
        groups = [_merge_top(stage(groups[i]), groups[i + 1], ties) for i in range(0, len(groups), 2)]
    return groups[0]


def _order_doubt(top, values):
    doubt = functools.reduce(lambda x, y: x + y,
                             [jnp.where(top[r][0] > top[r + 1][0], 0.0, 1.0) for r in range(len(top) - 1)])
    last = top[-1][0]
    count = functools.reduce(lambda x, y: x + y, [jnp.where(v >= last, 1.0, 0.0) for v in values])
    return doubt + jnp.where(count == float(len(top)), 0.0, 1.0)


def _product_top(top1, top2, stage=lambda items: items, ties=True):
    K = PEER_TOPK
    pairs = [(r1, r2) for r1 in range(K) for r2 in range(K) if (r1 + 1) * (r2 + 1) <= K]
    cands = [(top1[r1][0] + top2[r2][0], jnp.full_like(top1[0][0], float(r1 * K + r2))) for r1, r2 in pairs]
    pad = (jnp.full_like(top1[0][0], -jnp.inf), jnp.full_like(top1[0][0], float(K * K)))
    cands += [pad] * (-len(cands) % K)
    groups = [cands[:K]]
    for g in range(K, len(cands), K):
        groups.append(_sorted_group(stage(cands[g:g + K]), ties))
    best = _top_sorted(groups, stage, ties)
    doubt = None if ties else _order_doubt(best, [v for v, _ in cands[:len(pairs)]])
    top_s = [v for v, _ in best]
    e = [jnp.exp(v - top_s[0]) for v in top_s]
    z = functools.reduce(lambda x, y: x + y, e)
    gates, i1s, i2s = [], [], []
    for r in range(K):
        pos = best[r][1]
        r1 = jnp.floor(pos * np.float32(1.0 / K))
        r2 = pos - r1 * np.float32(K)
        i1, i2 = top1[0][1], top2[0][1]
        for a in range(1, K):
            i1 = jnp.where(r1 == float(a), top1[a][1], i1)
            i2 = jnp.where(r2 == float(a), top2[a][1], i2)
        gates.append(e[r] / z)
        i1s.append(i1)
        i2s.append(i2)
    return i1s, i2s, gates, doubt


def _route_gates_kernel(x_ref, wq_ref, sk_ref, g_ref, q_ref, sc_ref, slot_ref, tok_ref):
    K, n, half, H = PEER_TOPK, PEER_NKEYS, PEER_QDIM // 2, PEER_HEADS
    s = pl.program_id(0)
    n_chunks = pl.num_programs(0) - 1
    per_group = ROUTE_GROUP // ROUTE_CHUNK
    cur = s % 2

    @pl.when(s == 0)
    def _():
        slot_ref[...] = jnp.zeros_like(slot_ref)

    @pl.when(jnp.logical_and(s % per_group == 0, s < n_chunks))
    def _():
        q = jnp.dot(x_ref[...], wq_ref[...], preferred_element_type=F32)
        for hh in range(H):
            q_ref[hh] = q[:, hh * PEER_QDIM:(hh + 1) * PEER_QDIM].astype(BF16)

    for a in range(3):
        tok_ref[a] = slot_ref[1 - cur, a].T
    keys = lax.broadcasted_iota(jnp.int32, (n, n), 0).astype(F32)
    pieces = iter(range(ROUTE_CHUNK // GATE_PIECE))
    done = []

    def gate_piece():
        k = next(pieces, None)
        if k is None:
            return
        for t in range(k * GATE_PIECE, (k + 1) * GATE_PIECE):
            at = jnp.where(keys == tok_ref[0, t:t + 1, :], 1.0, 0.0).astype(BF16)
            bt = jnp.where(keys == tok_ref[1, t:t + 1, :], tok_ref[2, t:t + 1, :], 0.0).astype(BF16)
            g = lax.dot_general(at, bt, NT_DIMS, preferred_element_type=F32)
            g_ref[t] = g
        done.append(g[0:SUBLANES, :] * 0.0)

    def stage(items):
        gate_piece()
        if len(done) > GATE_LAG:
            (v, i), rest = items[0], list(items[1:])
            return [(v + done[len(done) - 1 - GATE_LAG], i)] + rest
        return items

    row0 = pl.multiple_of((s % per_group) * ROUTE_CHUNK, ROUTE_CHUNK)
    for p in range(2):
        for h in range(H):
            qc = q_ref[h, pl.ds(row0, ROUTE_CHUNK), p * half:(p + 1) * half]
            sc_ref[p, pl.ds(h, n, stride=H), :] = lax.dot_general(
                sk_ref[p], qc, NT_DIMS, preferred_element_type=F32)

    def route(ties, stage):
        tops, doubt = [], None
        for p in range(2):
            items = [(sc_ref[p, k * SUBLANES:(k + 1) * SUBLANES, :], jnp.full((SUBLANES, LANES), float(k), F32))
                     for k in range(n)]
            groups = [_sorted_group(stage(items[g:g + K]), ties) for g in range(0, n, K)]
            tops.append(_top_sorted(groups, stage, ties))
            if not ties:
                d = _order_doubt(tops[-1], [v for v, _ in items])
                doubt = d if doubt is None else doubt + d
        i1s, i2s, gates, d = _product_top(*tops, stage, ties)
        return (i1s, i2s, gates), (None if ties else doubt + d)

    def store(slots):
        for a, vals in enumerate(slots):
            for r in range(K):
                slot_ref[cur, a, pl.ds(r, H, stride=K), :] = vals[r]

    slots, doubt = route(False, stage)
    for _ in range(ROUTE_CHUNK // GATE_PIECE):
        gate_piece()
    store(slots)

    @pl.when(jnp.max(doubt) > 0.0)
    def _():
        store(route(True, lambda items: items)[0])


def _route_gates(xn2, wq, subkeys):
    T, D = xn2.shape
    n, slots = PEER_NKEYS, PEER_HEADS * PEER_TOPK
    n_chunks = T // ROUTE_CHUNK
    per_group = ROUTE_GROUP // ROUTE_CHUNK
    n_groups = T // ROUTE_GROUP
    once = pl.Buffered(1)
    return pl.pallas_call(
        _route_gates_kernel,
        grid=(n_chunks + 1,),
        in_specs=[
            pl.BlockSpec((ROUTE_GROUP, D), lambda s: (jnp.minimum(s // per_group, n_groups - 1), 0)),
            pl.BlockSpec(wq.shape, lambda s: (0, 0), pipeline_mode=once),
            pl.BlockSpec(subkeys.shape, lambda s: (0, 0, 0)),
        ],
        out_specs=pl.BlockSpec((ROUTE_CHUNK, n, n), lambda s: (jnp.maximum(s - 1, 0), 0, 0)),
        out_shape=jax.ShapeDtypeStruct((T, n, n), F32),
        scratch_shapes=[pltpu.VMEM((PEER_HEADS, ROUTE_GROUP, PEER_QDIM), BF16),
                        pltpu.VMEM((2, n * SUBLANES, LANES), F32),
                        pltpu.VMEM((2, 3, slots, ROUTE_CHUNK), F32),
                        pltpu.VMEM((3, ROUTE_CHUNK, slots), F32)],
        compiler_params=_params("arbitrary"),
        name="peer_route_gates",
    )(xn2, wq, subkeys)


EXPERT_G_ROWS = SUBLANES


def _experts_kernel(x_ref, u_ref, v_ref, g_ref, h1_ref, fg_ref, y_ref):
    e, k = pl.program_id(1), pl.program_id(2)
    first = jnp.logical_and(e == 0, k == 0)
    last = jnp.logical_and(e == pl.num_programs(1) - 1, k == pl.num_programs(2) - 1)

    @pl.when(first)
    def _():
        y_ref[...] = jnp.zeros_like(y_ref)

    u = u_ref[...].astype(BF16)
    v = v_ref[...].astype(BF16)
    act = _gelu(lax.dot_general(x_ref[...], u, NT_DIMS, preferred_element_type=F32))
    n = PEER_NKEYS
    tm, rows, _ = g_ref.shape
    nc = u_ref.shape[0] // n
    g_rows = g_ref.reshape(tm * rows, n)
    w = jnp.concatenate([(g_rows[pl.ds(k * nc + c, tm, stride=rows), :] * act[:, c * n:(c + 1) * n]).astype(BF16)
                         for c in range(nc)], axis=1)
    y_ref[...] += jnp.dot(w, v, preferred_element_type=F32)

    @pl.when(last)
    def _():
        h2 = h1_ref[...] + y_ref[...]
        y_ref[...] = h2 * _rms_scale(h2) * fg_ref[...]


def _experts(xn2, u_tab, v_tab, g3, h1, fg_row, tm=1024, te=512):
    T, D = xn2.shape
    E = u_tab.shape[0]
    n = PEER_NKEYS
    sub = EXPERT_G_ROWS * n // te
    once = pl.Buffered(1)
    return pl.pallas_call(
        _experts_kernel,
        grid=(T // tm, E // (te * sub), sub),
        in_specs=[
            pl.BlockSpec((tm, D), lambda i, e, k: (i, 0), pipeline_mode=once),
            pl.BlockSpec((te, D), lambda i, e, k: (e * sub + k, 0)),
            pl.BlockSpec((te, D), lambda i, e, k: (e * sub + k, 0)),
            pl.BlockSpec((tm, EXPERT_G_ROWS, n), lambda i, e, k: (i, e, 0)),
            pl.BlockSpec((tm, D), lambda i, e, k: (i, 0), pipeline_mode=once),
            pl.BlockSpec((1, D), lambda i, e, k: (0, 0)),
        ],
        out_specs=pl.BlockSpec((tm, D), lambda i, e, k: (i, 0), pipeline_mode=once),
        out_shape=jax.ShapeDtypeStruct((T, D), F32),
        compiler_params=_params("parallel", "arbitrary", "arbitrary"),
        name="peer_experts",
    )(xn2, u_tab, v_tab, g3, h1, fg_row)


def kernel(x, norm1_g, w_in, b_gates, conv_qk_w, mlstm_norm_g, s5_lambda_re, s5_lambda_im, s5_log_dt,
           s5_b_re, s5_b_im, s5_c_re, s5_c_im, s5_d, s5_glu_w, w_out, norm2_g, peer_wq, peer_subkeys,
           peer_u, peer_v, final_g):
    B, S, D = x.shape
    W, H = MLSTM_WIDTH, MLSTM_HEADS
    depth = norm1_g.shape[0]
    h = x.reshape(B * S, D)
    for l in range(depth):
        wl = w_in[l].astype(BF16)
        n_gate = 2 * H
        w_u = wl[:, 4 * W + n_gate:]
        w_gates = jnp.pad(wl[:, 4 * W:4 * W + n_gate], ((0, 0), (0, LANES - n_gate)))
        bias_row = jnp.pad(b_gates[l].astype(F32), (0, LANES - n_gate)).reshape(1, LANES)
        s5w = _s5_weights(s5_lambda_re[l].astype(F32), s5_lambda_im[l].astype(F32), s5_log_dt[l],
                          s5_b_re[l].astype(F32), s5_b_im[l].astype(F32), s5_c_re[l].astype(F32),
                          s5_c_im[l].astype(F32), s5_d[l], s5_glu_w[l].astype(F32))
        wo = w_out[l]

        z, gates = _inproj(h, norm1_g[l].reshape(1, D).astype(F32), wl, w_u, w_gates)
        hm = _mlstm(z, gates, bias_row, conv_qk_w[l].astype(F32),
                    mlstm_norm_g[l].reshape(1, W).astype(F32), B, S)
        ys = _s5(z, *s5w, B, S)
        h1, xn2 = _outproj(hm, ys, wo, h, norm2_g[l].reshape(1, D).astype(F32))

        g3 = _route_gates(xn2, peer_wq[l].astype(BF16), peer_subkeys[l].astype(BF16))
        assert l == depth - 1, "kernel is specialised to a single layer followed by the final norm"
        h = _experts(xn2, peer_u[l], peer_v[l], g3, h1,
                     final_g.reshape(1, D).astype(F32))
    return h.reshape(B, S, D)
```

```python
import functools
import math

import jax
import jax.numpy as jnp
import numpy as np
from jax import lax
from jax.experimental import pallas as pl
from jax.experimental.pallas import tpu as pltpu

F32 = jnp.float32
BF16 = jnp.bfloat16

D_MODEL = 2048
MLSTM_WIDTH = 1024
MLSTM_HEADS = 4
HEAD_DIM = 256
CHUNK = 128
CONV_WIDTH = 4
S5_WIDTH = 1024
S5_GROUP = 16
S5_GROUPS = 64
S5_STATE = 64
PEER_HEADS = 8
PEER_NKEYS = 128
PEER_EXPERTS = PEER_NKEYS * PEER_NKEYS
PEER_TOPK = 16
PEER_QDIM = 256
RMS_EPS = 1e-6

LANES = 128
SUBLANES = 8
VMEM_LIMIT = 56 * 1024 * 1024

NT_DIMS = (((1,), (1,)), ((), ()))


def _params(*sem):
    return pltpu.CompilerParams(dimension_semantics=sem, vmem_limit_bytes=VMEM_LIMIT)


def _rms_scale(v):
    return lax.rsqrt(jnp.mean(v * v, axis=-1, keepdims=True) + RMS_EPS)


def _gelu(v):
    return 0.5 * v * (1.0 + lax.erf(v * np.float32(math.sqrt(0.5))))


def _sigmoid(v):
    return 1.0 / (1.0 + jnp.exp(-v))


def _log_sigmoid(v):
    return jnp.minimum(v, 0.0) - jnp.log1p(jnp.exp(-jnp.abs(v)))


def _inproj_kernel(x_ref, g_ref, w_ref, wu_ref, wg_ref, z_ref, gates_ref, xn_ref):
    j = pl.program_id(1)
    last = pl.num_programs(1) - 1

    @pl.when(j == 0)
    def _():
        x = x_ref[...]
        xn_ref[...] = (x * _rms_scale(x) * g_ref[...]).astype(BF16)
        gates_ref[...] = lax.dot_general(xn_ref[...], wg_ref[...], NT_DIMS, preferred_element_type=F32)

    @pl.when(j < last)
    def _():
        z_ref[...] = lax.dot_general(xn_ref[...], w_ref[...], NT_DIMS, preferred_element_type=F32)

    @pl.when(j == last)
    def _():
        z_ref[...] = lax.dot_general(xn_ref[...], wu_ref[...], NT_DIMS, preferred_element_type=F32)


def _inproj(x2d, g_row, w_all, w_u, w_gates, tm=1024, tn=1024):
    T, D = x2d.shape
    n_main = (4 * MLSTM_WIDTH) // tn
    assert w_u.shape == (tn, D)
    return pl.pallas_call(
        _inproj_kernel,
        grid=(T // tm, n_main + 1),
        in_specs=[
            pl.BlockSpec((tm, D), lambda i, j: (i, 0)),
            pl.BlockSpec((1, D), lambda i, j: (0, 0)),
            pl.BlockSpec((tn, D), lambda i, j: (jnp.minimum(j, n_main - 1), 0)),
            pl.BlockSpec((tn, D), lambda i, j: (0, 0)),
            pl.BlockSpec((LANES, D), lambda i, j: (0, 0)),
        ],
        out_specs=[
            pl.BlockSpec((tm, tn), lambda i, j: (i, j)),
            pl.BlockSpec((tm, LANES), lambda i, j: (i, 0)),
        ],
        out_shape=[jax.ShapeDtypeStruct((T, (n_main + 1) * tn), F32), jax.ShapeDtypeStruct((T, LANES), F32)],
        scratch_shapes=[pltpu.VMEM((tm, D), BF16)],
        compiler_params=_params("parallel", "arbitrary"),
        name="inproj",
    )(x2d, g_row, w_all, w_u, w_gates)


def _mlstm_kernel(q_ref, k_ref, v_ref, o_ref, gates_ref, bias_ref, convw_ref, ng_ref, hm_ref,
                  xe_ref, c_ref, n_ref, m_ref):
    L, H, Dh, W = CHUNK, MLSTM_HEADS, HEAD_DIM, MLSTM_WIDTH
    tail = SUBLANES

    @pl.when(pl.program_id(1) == 0)
    def _():
        xe_ref[0:tail, :] = jnp.zeros((tail, 2 * W), F32)
        c_ref[...] = jnp.zeros_like(c_ref)
        n_ref[...] = jnp.zeros_like(n_ref)
        m_ref[...] = jnp.zeros_like(m_ref)

    xe_ref[tail:tail + L, 0:W] = q_ref[...]
    xe_ref[tail:tail + L, W:2 * W] = k_ref[...]
    ext = xe_ref[0:tail + L, :].astype(BF16)
    pick_r = lax.broadcasted_iota(jnp.int32, ((CONV_WIDTH - 1) * L, tail + L), 0)
    pick_c = lax.broadcasted_iota(jnp.int32, ((CONV_WIDTH - 1) * L, tail + L), 1)
    pick = jnp.where(pick_c == (pick_r % L) + (tail - (CONV_WIDTH - 1)) + pick_r // L, 1.0, 0.0).astype(BF16)
    shifted = jnp.dot(pick, ext, preferred_element_type=F32)
    conv = convw_ref[CONV_WIDTH - 1:CONV_WIDTH, :] * xe_ref[tail:tail + L, :]
    for j in range(CONV_WIDTH - 1):
        conv = conv + convw_ref[j:j + 1, :] * shifted[j * L:(j + 1) * L, :]
    xe_ref[0:tail, :] = xe_ref[L:L + tail, :]
    qk = conv * _sigmoid(conv)

    g = gates_ref[...] + bias_ref[...]
    g_t = g.T
    row = lax.broadcasted_iota(jnp.int32, (L, L), 0)
    col = lax.broadcasted_iota(jnp.int32, (L, L), 1)
    causal = col <= row
    tri = jnp.where(causal, 1.0, 0.0).astype(F32)
    tri_t = jnp.where(row <= col, 1.0, 0.0).astype(F32)
    def split(v):
        hi = v.astype(BF16)
        return hi, (v - hi.astype(F32)).astype(BF16)

    tri_b, tri_tb = tri.astype(BF16), tri_t.astype(BF16)
    c_hi, c_lo = split(_log_sigmoid(g))
    a_cols = (jnp.dot(tri_b, c_hi, preferred_element_type=F32)
              + jnp.dot(tri_b, c_lo, preferred_element_type=F32))
    r_hi, r_lo = split(_log_sigmoid(g_t))
    a_rows = (jnp.dot(r_hi, tri_tb, preferred_element_type=F32)
              + jnp.dot(r_lo, tri_tb, preferred_element_type=F32))

    scale = np.float32(Dh ** -0.5)
    for h in range(H):
        a_c = a_cols[:, H + h:H + h + 1]
        a_r = a_rows[H + h:H + h + 1, :]
        li_c = g[:, h:h + 1]
        li_r = g_t[h:h + 1, :]
        a_tot = a_r[:, L - 1:L]
        m0 = m_ref[h]
        c0 = c_ref[h]
        n0 = n_ref[h]

        q = qk[:, h * Dh:(h + 1) * Dh] * scale
        k = qk[:, W + h * Dh:W + (h + 1) * Dh]
        v = v_ref[:, h * Dh:(h + 1) * Dh]
        qb, kb, vb = q.astype(BF16), k.astype(BF16), v.astype(BF16)

        dmat = jnp.where(causal, a_c - a_r + li_r, -jnp.inf)
        m_inter = a_c + m0
        m_t = jnp.maximum(m_inter, jnp.max(dmat, axis=-1, keepdims=True))
        wm = jnp.exp(dmat - m_t)
        s_inter = jnp.exp(m_inter - m_t)
        s = lax.dot_general(qb, kb, NT_DIMS, preferred_element_type=F32) * wm
        num = (jnp.dot(s.astype(BF16), vb, preferred_element_type=F32)
               + s_inter * jnp.dot(qb, c0.astype(BF16), preferred_element_type=F32))
        den = (jnp.sum(s, axis=-1, keepdims=True)
               + s_inter * jnp.sum(q * n0, axis=-1, keepdims=True))
        hh = num / jnp.maximum(jnp.abs(den), jnp.exp(-m_t))
        hh = hh * _rms_scale(hh)
        og = _sigmoid(o_ref[:, h * Dh:(h + 1) * Dh])
        hm_ref[:, h * Dh:(h + 1) * Dh] = (hh * ng_ref[:, h * Dh:(h + 1) * Dh] * og).astype(BF16)

        g_r = a_tot - a_r + li_r
        g_c = a_tot - a_c + li_c
        m_loc = jnp.max(g_r, axis=-1, keepdims=True)
        kw = k * jnp.exp(g_c - m_loc)
        d_c = jnp.dot(kw.T.astype(BF16), vb, preferred_element_type=F32)
        d_n = jnp.sum(kw, axis=0, keepdims=True)
        m_new = jnp.maximum(a_tot + m0, m_loc)
        s_old = jnp.exp(a_tot + m0 - m_new)
        s_new = jnp.exp(m_loc - m_new)
        c_ref[h] = s_old * c0 + s_new * d_c
        n_ref[h] = s_old * n0 + s_new * d_n
        m_ref[h] = m_new


def _mlstm(z, gates, bias_row, conv_w, norm_g_row, batch, seq):
    T = z.shape[0]
    L, W = CHUNK, MLSTM_WIDTH
    nc = seq // L
    blk = lambda col: pl.BlockSpec((L, W), lambda b, c, col=col: (b * nc + c, col))
    return pl.pallas_call(
        _mlstm_kernel,
        grid=(batch, nc),
        in_specs=[
            blk(0), blk(1), blk(2), blk(3),
            pl.BlockSpec((L, LANES), lambda b, c: (b * nc + c, 0)),
            pl.BlockSpec((1, LANES), lambda b, c: (0, 0)),
            pl.BlockSpec((CONV_WIDTH, 2 * W), lambda b, c: (0, 0)),
            pl.BlockSpec((1, W), lambda b, c: (0, 0)),
        ],
        out_specs=pl.BlockSpec((L, W), lambda b, c: (b * nc + c, 0)),
        out_shape=jax.ShapeDtypeStruct((T, W), BF16),
        scratch_shapes=[
            pltpu.VMEM((L + 2 * SUBLANES, 2 * W), F32),
            pltpu.VMEM((MLSTM_HEADS, HEAD_DIM, HEAD_DIM), F32),
            pltpu.VMEM((MLSTM_HEADS, 1, HEAD_DIM), F32),
            pltpu.VMEM((MLSTM_HEADS, 1, 1), F32),
        ],
        compiler_params=_params("parallel", "arbitrary"),
        name="mlstm",
    )(z, z, z, z, gates, bias_row, conv_w, norm_g_row)


S5_BLOCK = 256
S5_SEG = S5_BLOCK // SUBLANES
S5_TILES = S5_WIDTH // LANES
S5_TILE_STATE = 8 * S5_STATE


def _s5_kernel(u_ref, win_ref, wout_ref, wglu_ref, a_ref, aseg_ref, d_ref, ys_ref,
               up_ref, x_ref, xs_ref, yp_ref, carry_ref):
    nseg, seg, half = SUBLANES, S5_SEG, S5_TILE_STATE
    chains = range(u_ref.shape[0])

    @pl.when(pl.program_id(1) == 0)
    def _():
        carry_ref[...] = jnp.zeros_like(carry_ref)

    def cmul_add(ar, ai, sr, si, xr, xi):
        return ar * sr - ai * si + xr, ar * si + ai * sr + xi

    def x_rows(b, i):
        rows = slice(nseg * i, nseg * (i + 1))
        return x_ref[b, rows, 0:half], x_ref[b, rows, half:2 * half]

    for i in range(seg):
        for b in chains:
            up_ref[b, nseg * i:nseg * (i + 1), :] = u_ref[b, pl.ds(i, nseg, stride=seg), :]
    for b in chains:
        x_ref[b] = jnp.dot(up_ref[b].astype(BF16), win_ref[...], preferred_element_type=F32)
    ar = a_ref[:, 0:half]
    ai = a_ref[:, half:2 * half]

    fin = [(jnp.zeros((nseg, half), F32), jnp.zeros((nseg, half), F32)) for _ in chains]
    for i in range(seg):
        for b in chains:
            fin[b] = cmul_add(ar, ai, *fin[b], *x_rows(b, i))

    lr = aseg_ref[:, 0:half]
    lim = aseg_ref[:, half:2 * half]
    state = []
    for b in chains:
        cr = carry_ref[b, :, 0:half]
        ci = carry_ref[b, :, half:2 * half]
        crs, cis = [], []
        for r in range(nseg):
            crs.append(cr)
            cis.append(ci)
            cr, ci = cmul_add(lr, lim, cr, ci, fin[b][0][r:r + 1, :], fin[b][1][r:r + 1, :])
        carry_ref[b, :, 0:half] = cr
        carry_ref[b, :, half:2 * half] = ci
        state.append((jnp.concatenate(crs, axis=0), jnp.concatenate(cis, axis=0)))

    for i in range(seg):
        rows = slice(nseg * i, nseg * (i + 1))
        for b in chains:
            state[b] = cmul_add(ar, ai, *state[b], *x_rows(b, i))
            xs_ref[b, rows, 0:half] = state[b][0]
            xs_ref[b, rows, half:2 * half] = state[b][1]

    for b in chains:
        y = jnp.dot(xs_ref[b].astype(BF16), wout_ref[...], preferred_element_type=F32)
        y = y + d_ref[...] * up_ref[b]
        ab = jnp.dot(_gelu(y).astype(BF16), wglu_ref[...], preferred_element_type=F32)
        yp_ref[b] = ab[:, 0:LANES] * _sigmoid(ab[:, LANES:2 * LANES])
    for i in range(seg):
        for b in chains:
            ys_ref[b, pl.ds(i, nseg, stride=seg), :] = yp_ref[b, nseg * i:nseg * (i + 1), :]


def _s5(z, win, wout, wglu, a_rows, aseg_rows, d_row, batch, seq):
    T, N = z.shape
    blk = S5_BLOCK
    nb = seq // blk
    u_col = (4 * MLSTM_WIDTH) // LANES
    tile = lambda shape: pl.BlockSpec((None,) + shape, lambda j, c: (j,) + (0,) * len(shape))
    ys = pl.pallas_call(
        _s5_kernel,
        grid=(S5_TILES, nb),
        in_specs=[
            pl.BlockSpec((batch, blk, LANES), lambda j, c: (0, c, u_col + j)),
            tile(win.shape[1:]), tile(wout.shape[1:]), tile(wglu.shape[1:]),
            tile(a_rows.shape[1:]), tile(aseg_rows.shape[1:]),
            pl.BlockSpec((1, LANES), lambda j, c: (0, j)),
        ],
        out_specs=pl.BlockSpec((batch, blk, LANES), lambda j, c: (0, c, j)),
        out_shape=jax.ShapeDtypeStruct((batch, seq, S5_WIDTH), F32),
        scratch_shapes=[
            pltpu.VMEM((batch, blk, LANES), F32),
            pltpu.VMEM((batch, blk, 2 * S5_TILE_STATE), F32),
            pltpu.VMEM((batch, blk, 2 * S5_TILE_STATE), F32),
            pltpu.VMEM((batch, blk, LANES), F32),
            pltpu.VMEM((batch, 1, 2 * S5_TILE_STATE), F32),
        ],
        compiler_params=_params("parallel", "arbitrary"),
        name="s5",
    )(z.reshape(batch, seq, N), win, wout, wglu, a_rows, aseg_rows, d_row)
    return ys.reshape(T, S5_WIDTH)


def _s5_weights(lam_re, lam_im, log_dt, b_re, b_im, c_re, c_im, d, glu_w):
    G, P, Hg = S5_GROUPS, S5_STATE, S5_GROUP
    nt, gl = S5_TILES, G // S5_TILES
    dt = jnp.exp(log_dt.astype(F32))[:, None]
    mag = jnp.exp(lam_re * dt)
    ar = mag * jnp.cos(lam_im * dt)
    ai = mag * jnp.sin(lam_im * dt)
    den = lam_re * lam_re + lam_im * lam_im
    fr = ((ar - 1.0) * lam_re + ai * lam_im) / den
    fi = (ai * lam_re - (ar - 1.0) * lam_im) / den
    bbr = fr[..., None] * b_re - fi[..., None] * b_im
    bbi = fr[..., None] * b_im + fi[..., None] * b_re
    eye = jnp.eye(gl, dtype=F32)

    def tiles(v):
        return v.reshape((nt, gl) + v.shape[1:])

    bb = jnp.stack([tiles(bbr), tiles(bbi)], axis=2)
    win = jnp.einsum('tgrph,gk->tghrkp', bb, eye).reshape(nt, gl * Hg, 2 * gl * P)
    cc = jnp.stack([tiles(c_re), -tiles(c_im)], axis=2)
    wout = jnp.einsum('tgrhp,gk->trkpgh', cc, eye).reshape(nt, 2 * gl * P, gl * Hg)
    gw = tiles(glu_w).reshape(nt, gl, Hg, 2, Hg)
    wglu = jnp.einsum('tghrk,gq->tqhrgk', gw, eye).reshape(nt, gl * Hg, 2 * gl * Hg)

    def rows(re, im):
        return jnp.concatenate([tiles(re).reshape(nt, 1, gl * P), tiles(im).reshape(nt, 1, gl * P)], axis=-1)

    a_rows = jnp.broadcast_to(rows(ar, ai), (nt, SUBLANES, 2 * gl * P))
    pr, pi = ar, ai
    for _ in range(int(math.log2(S5_SEG))):
        pr, pi = pr * pr - pi * pi, 2.0 * pr * pi
    aseg_rows = rows(pr, pi)
    d_row = d.reshape(1, G * Hg).astype(F32)
    return win.astype(BF16), wout.astype(BF16), wglu.astype(BF16), a_rows, aseg_rows, d_row


def _outproj_kernel(hm_ref, ys_ref, w1_ref, w2_ref, x_ref, g_ref, h1_ref, xn_ref):
    mix = (jnp.dot(hm_ref[...], w1_ref[...].astype(BF16), preferred_element_type=F32)
           + jnp.dot(ys_ref[...].astype(BF16), w2_ref[...].astype(BF16), preferred_element_type=F32))
    h1 = x_ref[...] + mix
    h1_ref[...] = h1
    xn_ref[...] = (h1 * _rms_scale(h1) * g_ref[...]).astype(BF16)


def _outproj(hm, ys, w, x2d, g_row, tm=512):
    T, D = x2d.shape
    W = hm.shape[1]
    row = lambda w: pl.BlockSpec((tm, w), lambda i: (i, 0))
    half = lambda k: pl.BlockSpec((W, D), lambda i: (k, 0), pipeline_mode=pl.Buffered(1))
    return pl.pallas_call(
        _outproj_kernel,
        grid=(T // tm,),
        in_specs=[row(W), row(W), half(0), half(1), row(D), pl.BlockSpec((1, D), lambda i: (0, 0))],
        out_specs=[row(D), row(D)],
        out_shape=[jax.ShapeDtypeStruct((T, D), F32), jax.ShapeDtypeStruct((T, D), BF16)],
        compiler_params=_params("parallel"),
        name="outproj",
    )(hm, ys, w, w, x2d, g_row)


ROUTE_GROUP = 1024
ROUTE_CHUNK = LANES
GATE_PIECE = 4
GATE_LAG = 1
assert PEER_HEADS == SUBLANES


def _sort_pairs(n):
    pairs = []
    p = 1
    while p < n:
        k = p
        while k >= 1:
            for j in range(k % p, n - k, 2 * k):
                for i in range(min(k, n - j - k)):
                    if (i + j) // (2 * p) == (i + j + k) // (2 * p):
                        pairs.append((i + j, i + j + k))
            k //= 2
        p *= 2
    return pairs


def _first_second(a, b, ties=True):
    (av, ai), (bv, bi) = a, b
    if ties:
        tie = av == bv
        a_first = jnp.where(tie, bi, av) > jnp.where(tie, ai, bv)
    else:
        a_first = av > bv
    return ((jnp.maximum(av, bv), jnp.where(a_first, ai, bi)),
            (jnp.minimum(av, bv), jnp.where(a_first, bi, ai)))


def _sorted_group(items, ties=True):
    items = list(items)
    for i, j in _sort_pairs(len(items)):
        items[i], items[j] = _first_second(items[i], items[j], ties)
    return items


def _merge_top(a, b, ties=True):
    n = len(a)
    c = [_first_second(a[v], b[n - 1 - v], ties)[0] for v in range(n)]
    j = n // 2
    while j >= 1:
        for i in range(n):
            if (i & j) == 0:
                c[i], c[i + j] = _first_second(c[i], c[i + j], ties)
        j //= 2
    return c


def _top_sorted(groups, stage=lambda items: items, ties=True):
    groups = list(groups)
    while len(groups) > 1:
        groups = [_merge_top(stage(groups[i]), groups[i + 1], ties) for i in range(0, len(groups), 2)]
    return groups[0]


def _order_doubt(top, values):
    doubt = functools.reduce(lambda x, y: x + y,
                             [jnp.where(top[r][0] > top[r + 1][0], 0.0, 1.0) for r in range(len(top) - 1)])
    last = top[-1][0]
    count = functools.reduce(lambda x, y: x + y, [jnp.where(v >= last, 1.0, 0.0) for v in values])
    return doubt + jnp.where(count == float(len(top)), 0.0, 1.0)


def _product_top(top1, top2, stage=lambda items: items, ties=True):
    K = PEER_TOPK
    pairs = [(r1, r2) for r1 in range(K) for r2 in range(K) if (r1 + 1) * (r2 + 1) <= K]
    cands = [(top1[r1][0] + top2[r2][0], jnp.full_like(top1[0][0], float(r1 * K + r2))) for r1, r2 in pairs]
    pad = (jnp.full_like(top1[0][0], -jnp.inf), jnp.full_like(top1[0][0], float(K * K)))
    cands += [pad] * (-len(cands) % K)
    groups = [cands[:K]]
    for g in range(K, len(cands), K):
        groups.append(_sorted_group(stage(cands[g:g + K]), ties))
    best = _top_sorted(groups, stage, ties)
    doubt = None if ties else _order_doubt(best, [v for v, _ in cands[:len(pairs)]])
    top_s = [v for v, _ in best]
    e = [jnp.exp(v - top_s[0]) for v in top_s]
    z = functools.reduce(lambda x, y: x + y, e)
    gates, i1s, i2s = [], [], []
    for r in range(K):
        pos = best[r][1]
        r1 = jnp.floor(pos * np.float32(1.0 / K))
        r2 = pos - r1 * np.float32(K)
        i1, i2 = top1[0][1], top2[0][1]
        for a in range(1, K):
            i1 = jnp.where(r1 == float(a), top1[a][1], i1)
            i2 = jnp.where(r2 == float(a), top2[a][1], i2)
        gates.append(e[r] / z)
        i1s.append(i1)
        i2s.append(i2)
    return i1s, i2s, gates, doubt


def _route_gates_kernel(x_ref, wq_ref, sk_ref, g_ref, q_ref, sc_ref, slot_ref, tok_ref):
    K, n, half, H = PEER_TOPK, PEER_NKEYS, PEER_QDIM // 2, PEER_HEADS
    s = pl.program_id(0)
    n_chunks = pl.num_programs(0) - 1
    per_group = ROUTE_GROUP // ROUTE_CHUNK
    cur = s % 2

    @pl.when(s == 0)
    def _():
        slot_ref[...] = jnp.zeros_like(slot_ref)

    @pl.when(jnp.logical_and(s % per_group == 0, s < n_chunks))
    def _():
        q = jnp.dot(x_ref[...], wq_ref[...], preferred_element_type=F32)
        for hh in range(H):
            q_ref[hh] = q[:, hh * PEER_QDIM:(hh + 1) * PEER_QDIM].astype(BF16)

    for a in range(3):
        tok_ref[a] = slot_ref[1 - cur, a].T
    keys = lax.broadcasted_iota(jnp.int32, (n, n), 0).astype(F32)
    pieces = iter(range(ROUTE_CHUNK // GATE_PIECE))
    done = []

    def gate_piece():
        k = next(pieces, None)
        if k is None:
            return
        for t in range(k * GATE_PIECE, (k + 1) * GATE_PIECE):
            at = jnp.where(keys == tok_ref[0, t:t + 1, :], 1.0, 0.0).astype(BF16)
            bt = jnp.where(keys == tok_ref[1, t:t + 1, :], tok_ref[2, t:t + 1, :], 0.0).astype(BF16)
            g = lax.dot_general(at, bt, NT_DIMS, preferred_element_type=F32)
            g_ref[t] = g
        done.append(g[0:SUBLANES, :] * 0.0)

    def stage(items):
        gate_piece()
        if len(done) > GATE_LAG:
            (v, i), rest = items[0], list(items[1:])
            return [(v + done[len(done) - 1 - GATE_LAG], i)] + rest
        return items

    row0 = pl.multiple_of((s % per_group) * ROUTE_CHUNK, ROUTE_CHUNK)
    for p in range(2):
        for h in range(H):
            qc = q_ref[h, pl.ds(row0, ROUTE_CHUNK), p * half:(p + 1) * half]
            sc_ref[p, pl.ds(h, n, stride=H), :] = lax.dot_general(
                sk_ref[p], qc, NT_DIMS, preferred_element_type=F32)

    def route(ties, stage):
        tops, doubt = [], None
        for p in range(2):
            items = [(sc_ref[p, k * SUBLANES:(k + 1) * SUBLANES, :], jnp.full((SUBLANES, LANES), float(k), F32))
                     for k in range(n)]
            groups = [_sorted_group(stage(items[g:g + K]), ties) for g in range(0, n, K)]
            tops.append(_top_sorted(groups, stage, ties))
            if not ties:
                d = _order_doubt(tops[-1], [v for v, _ in items])
                doubt = d if doubt is None else doubt + d
        i1s, i2s, gates, d = _product_top(*tops, stage, ties)
        return (i1s, i2s, gates), (None if ties else doubt + d)

    def store(slots):
        for a, vals in enumerate(slots):
            for r in range(K):
                slot_ref[cur, a, pl.ds(r, H, stride=K), :] = vals[r]

    slots, doubt = route(False, stage)
    for _ in range(ROUTE_CHUNK // GATE_PIECE):
        gate_piece()
    store(slots)

    @pl.when(jnp.max(doubt) > 0.0)
    def _():
        store(route(True, lambda items: items)[0])


def _route_gates(xn2, wq, subkeys):
    T, D = xn2.shape
    n, slots = PEER_NKEYS, PEER_HEADS * PEER_TOPK
    n_chunks = T // ROUTE_CHUNK
    per_group = ROUTE_GROUP // ROUTE_CHUNK
    n_groups = T // ROUTE_GROUP
    once = pl.Buffered(1)
    return pl.pallas_call(
        _route_gates_kernel,
        grid=(n_chunks + 1,),
        in_specs=[
            pl.BlockSpec((ROUTE_GROUP, D), lambda s: (jnp.minimum(s // per_group, n_groups - 1), 0)),
            pl.BlockSpec(wq.shape, lambda s: (0, 0), pipeline_mode=once),
            pl.BlockSpec(subkeys.shape, lambda s: (0, 0, 0)),
        ],
        out_specs=pl.BlockSpec((ROUTE_CHUNK, n, n), lambda s: (jnp.maximum(s - 1, 0), 0, 0)),
        out_shape=jax.ShapeDtypeStruct((T, n, n), F32),
        scratch_shapes=[pltpu.VMEM((PEER_HEADS, ROUTE_GROUP, PEER_QDIM), BF16),
                        pltpu.VMEM((2, n * SUBLANES, LANES), F32),
                        pltpu.VMEM((2, 3, slots, ROUTE_CHUNK), F32),
                        pltpu.VMEM((3, ROUTE_CHUNK, slots), F32)],
        compiler_params=_params("arbitrary"),
        name="peer_route_gates",
    )(xn2, wq, subkeys)


EXPERT_G_ROWS = SUBLANES


def _experts_kernel(x_ref, u_ref, v_ref, g_ref, h1_ref, fg_ref, y_ref):
    e, k = pl.program_id(1), pl.program_id(2)
    first = jnp.logical_and(e == 0, k == 0)
    last = jnp.logical_and(e == pl.num_programs(1) - 1, k == pl.num_programs(2) - 1)

    @pl.when(first)
    def _():
        y_ref[...] = jnp.zeros_like(y_ref)

    u = u_ref[...].astype(BF16)
    v = v_ref[...].astype(BF16)
    act = _gelu(lax.dot_general(x_ref[...], u, NT_DIMS, preferred_element_type=F32))
    n = PEER_NKEYS
    tm, rows, _ = g_ref.shape
    nc = u_ref.shape[0] // n
    g_rows = g_ref.reshape(tm * rows, n)
    w = jnp.concatenate([(g_rows[pl.ds(k * nc + c, tm, stride=rows), :] * act[:, c * n:(c + 1) * n]).astype(BF16)
                         for c in range(nc)], axis=1)
    y_ref[...] += jnp.dot(w, v, preferred_element_type=F32)

    @pl.when(last)
    def _():
        h2 = h1_ref[...] + y_ref[...]
        y_ref[...] = h2 * _rms_scale(h2) * fg_ref[...]


def _experts(xn2, u_tab, v_tab, g3, h1, fg_row, tm=1024, te=512):
    T, D = xn2.shape
    E = u_tab.shape[0]
    n = PEER_NKEYS
    sub = EXPERT_G_ROWS * n // te
    once = pl.Buffered(1)
    return pl.pallas_call(
        _experts_kernel,
        grid=(T // tm, E // (te * sub), sub),
        in_specs=[
            pl.BlockSpec((tm, D), lambda i, e, k: (i, 0), pipeline_mode=once),
            pl.BlockSpec((te, D), lambda i, e, k: (e * sub + k, 0)),
            pl.BlockSpec((te, D), lambda i, e, k: (e * sub + k, 0)),
            pl.BlockSpec((tm, EXPERT_G_ROWS, n), lambda i, e, k: (i, e, 0)),
            pl.BlockSpec((tm, D), lambda i, e, k: (i, 0), pipeline_mode=once),
            pl.BlockSpec((1, D), lambda i, e, k: (0, 0)),
        ],
        out_specs=pl.BlockSpec((tm, D), lambda i, e, k: (i, 0), pipeline_mode=once),
        out_shape=jax.ShapeDtypeStruct((T, D), F32),
        compiler_params=_params("parallel", "arbitrary", "arbitrary"),
        name="peer_experts",
    )(xn2, u_tab, v_tab, g3, h1, fg_row)


def kernel(x, norm1_g, w_in, b_gates, conv_qk_w, mlstm_norm_g, s5_lambda_re, s5_lambda_im, s5_log_dt,
           s5_b_re, s5_b_im, s5_c_re, s5_c_im, s5_d, s5_glu_w, w_out, norm2_g, peer_wq, peer_subkeys,
           peer_u, peer_v, final_g):
    B, S, D = x.shape
    W, H = MLSTM_WIDTH, MLSTM_HEADS
    depth = norm1_g.shape[0]
    h = x.reshape(B * S, D)
    for l in range(depth):
        wl = jnp.swapaxes(w_in[l], 0, 1).astype(BF16)
        n_gate = 2 * H
        w_u = wl[4 * W + n_gate:]
        w_gates = jnp.pad(wl[4 * W:4 * W + n_gate], ((0, LANES - n_gate), (0, 0)))
        bias_row = jnp.pad(b_gates[l].astype(F32), (0, LANES - n_gate)).reshape(1, LANES)
        s5w = _s5_weights(s5_lambda_re[l].astype(F32), s5_lambda_im[l].astype(F32), s5_log_dt[l],
                          s5_b_re[l].astype(F32), s5_b_im[l].astype(F32), s5_c_re[l].astype(F32),
                          s5_c_im[l].astype(F32), s5_d[l], s5_glu_w[l].astype(F32))
        wo = w_out[l]

        z, gates = _inproj(h, norm1_g[l].reshape(1, D).astype(F32), wl, w_u, w_gates)
        hm = _mlstm(z, gates, bias_row, conv_qk_w[l].astype(F32),
                    mlstm_norm_g[l].reshape(1, W).astype(F32), B, S)
        ys = _s5(z, *s5w, B, S)
        h1, xn2 = _outproj(hm, ys, wo, h, norm2_g[l].reshape(1, D).astype(F32))

        g3 = _route_gates(xn2, peer_wq[l].astype(BF16), peer_subkeys[l].astype(BF16))
        assert l == depth - 1, "kernel is specialised to a single layer followed by the final norm"
        h = _experts(xn2, peer_u[l], peer_v[l], g3, h1,
                     final_g.reshape(1, D).astype(F32))
    return h.reshape(B, S, D)
```

```python
import functools
import math

import jax
import jax.numpy as jnp
import numpy as np
from jax import lax
from jax.experimental import pallas as pl
from jax.experimental.pallas import tpu as pltpu

F32 = jnp.float32
BF16 = jnp.bfloat16

D_MODEL = 2048
MLSTM_WIDTH = 1024
MLSTM_HEADS = 4
HEAD_DIM = 256
CHUNK = 128
CONV_WIDTH = 4
S5_WIDTH = 1024
S5_GROUP = 16
S5_GROUPS = 64
S5_STATE = 64
PEER_HEADS = 8
PEER_NKEYS = 128
PEER_EXPERTS = PEER_NKEYS * PEER_NKEYS
PEER_TOPK = 16
PEER_QDIM = 256
RMS_EPS = 1e-6

LANES = 128
SUBLANES = 8
VMEM_LIMIT = 56 * 1024 * 1024

NT_DIMS = (((1,), (1,)), ((), ()))


def _params(*sem):
    return pltpu.CompilerParams(dimension_semantics=sem, vmem_limit_bytes=VMEM_LIMIT)


def _rms_scale(v):
    return lax.rsqrt(jnp.mean(v * v, axis=-1, keepdims=True) + RMS_EPS)


def _gelu(v):
    return 0.5 * v * (1.0 + lax.erf(v * np.float32(math.sqrt(0.5))))


def _sigmoid(v):
    return 1.0 / (1.0 + jnp.exp(-v))


def _log_sigmoid(v):
    return jnp.minimum(v, 0.0) - jnp.log1p(jnp.exp(-jnp.abs(v)))


def _inproj_kernel(x_ref, g_ref, w_ref, wu_ref, wg_ref, z_ref, gates_ref, xn_ref):
    j = pl.program_id(1)
    last = pl.num_programs(1) - 1

    @pl.when(j == 0)
    def _():
        x = x_ref[...]
        xn_ref[...] = (x * _rms_scale(x) * g_ref[...]).astype(BF16)
        gates_ref[...] = lax.dot_general(xn_ref[...], wg_ref[...], NT_DIMS, preferred_element_type=F32)

    @pl.when(j < last)
    def _():
        z_ref[...] = lax.dot_general(xn_ref[...], w_ref[...], NT_DIMS, preferred_element_type=F32)

    @pl.when(j == last)
    def _():
        z_ref[...] = lax.dot_general(xn_ref[...], wu_ref[...], NT_DIMS, preferred_element_type=F32)


def _inproj(x2d, g_row, w_all, w_u, w_gates, tm=1024, tn=1024):
    T, D = x2d.shape
    n_main = (4 * MLSTM_WIDTH) // tn
    assert w_u.shape == (tn, D)
    return pl.pallas_call(
        _inproj_kernel,
        grid=(T // tm, n_main + 1),
        in_specs=[
            pl.BlockSpec((tm, D), lambda i, j: (i, 0)),
            pl.BlockSpec((1, D), lambda i, j: (0, 0)),
            pl.BlockSpec((tn, D), lambda i, j: (jnp.minimum(j, n_main - 1), 0)),
            pl.BlockSpec((tn, D), lambda i, j: (0, 0)),
            pl.BlockSpec((LANES, D), lambda i, j: (0, 0)),
        ],
        out_specs=[
            pl.BlockSpec((tm, tn), lambda i, j: (i, j)),
            pl.BlockSpec((tm, LANES), lambda i, j: (i, 0)),
        ],
        out_shape=[jax.ShapeDtypeStruct((T, (n_main + 1) * tn), F32), jax.ShapeDtypeStruct((T, LANES), F32)],
        scratch_shapes=[pltpu.VMEM((tm, D), BF16)],
        compiler_params=_params("parallel", "arbitrary"),
        name="inproj",
    )(x2d, g_row, w_all, w_u, w_gates)


def _mlstm_kernel(q_ref, k_ref, v_ref, o_ref, gates_ref, bias_ref, convw_ref, ng_ref, hm_ref,
                  xe_ref, c_ref, n_ref, m_ref):
    L, H, Dh, W = CHUNK, MLSTM_HEADS, HEAD_DIM, MLSTM_WIDTH
    tail = SUBLANES

    @pl.when(pl.program_id(1) == 0)
    def _():
        xe_ref[0:tail, :] = jnp.zeros((tail, 2 * W), F32)
        c_ref[...] = jnp.zeros_like(c_ref)
        n_ref[...] = jnp.zeros_like(n_ref)
        m_ref[...] = jnp.zeros_like(m_ref)

    xe_ref[tail:tail + L, 0:W] = q_ref[...]
    xe_ref[tail:tail + L, W:2 * W] = k_ref[...]
    ext = xe_ref[0:tail + L, :].astype(BF16)
    pick_r = lax.broadcasted_iota(jnp.int32, ((CONV_WIDTH - 1) * L, tail + L), 0)
    pick_c = lax.broadcasted_iota(jnp.int32, ((CONV_WIDTH - 1) * L, tail + L), 1)
    pick = jnp.where(pick_c == (pick_r % L) + (tail - (CONV_WIDTH - 1)) + pick_r // L, 1.0, 0.0).astype(BF16)
    shifted = jnp.dot(pick, ext, preferred_element_type=F32)
    conv = convw_ref[CONV_WIDTH - 1:CONV_WIDTH, :] * xe_ref[tail:tail + L, :]
    for j in range(CONV_WIDTH - 1):
        conv = conv + convw_ref[j:j + 1, :] * shifted[j * L:(j + 1) * L, :]
    xe_ref[0:tail, :] = xe_ref[L:L + tail, :]
    qk = conv * _sigmoid(conv)

    g = gates_ref[...] + bias_ref[...]
    g_t = g.T
    row = lax.broadcasted_iota(jnp.int32, (L, L), 0)
    col = lax.broadcasted_iota(jnp.int32, (L, L), 1)
    causal = col <= row
    tri = jnp.where(causal, 1.0, 0.0).astype(F32)
    tri_t = jnp.where(row <= col, 1.0, 0.0).astype(F32)
    def split(v):
        hi = v.astype(BF16)
        return hi, (v - hi.astype(F32)).astype(BF16)

    tri_b, tri_tb = tri.astype(BF16), tri_t.astype(BF16)
    c_hi, c_lo = split(_log_sigmoid(g))
    a_cols = (jnp.dot(tri_b, c_hi, preferred_element_type=F32)
              + jnp.dot(tri_b, c_lo, preferred_element_type=F32))
    r_hi, r_lo = split(_log_sigmoid(g_t))
    a_rows = (jnp.dot(r_hi, tri_tb, preferred_element_type=F32)
              + jnp.dot(r_lo, tri_tb, preferred_element_type=F32))

    scale = np.float32(Dh ** -0.5)
    for h in range(H):
        a_c = a_cols[:, H + h:H + h + 1]
        a_r = a_rows[H + h:H + h + 1, :]
        li_c = g[:, h:h + 1]
        li_r = g_t[h:h + 1, :]
        a_tot = a_r[:, L - 1:L]
        m0 = m_ref[h]
        c0 = c_ref[h]
        n0 = n_ref[h]

        q = qk[:, h * Dh:(h + 1) * Dh] * scale
        k = qk[:, W + h * Dh:W + (h + 1) * Dh]
        v = v_ref[:, h * Dh:(h + 1) * Dh]
        qb, kb, vb = q.astype(BF16), k.astype(BF16), v.astype(BF16)

        dmat = jnp.where(causal, a_c - a_r + li_r, -jnp.inf)
        m_inter = a_c + m0
        m_t = jnp.maximum(m_inter, jnp.max(dmat, axis=-1, keepdims=True))
        wm = jnp.exp(dmat - m_t)
        s_inter = jnp.exp(m_inter - m_t)
        s = lax.dot_general(qb, kb, NT_DIMS, preferred_element_type=F32) * wm
        num = (jnp.dot(s.astype(BF16), vb, preferred_element_type=F32)
               + s_inter * jnp.dot(qb, c0.astype(BF16), preferred_element_type=F32))
        den = (jnp.sum(s, axis=-1, keepdims=True)
               + s_inter * jnp.sum(q * n0, axis=-1, keepdims=True))
        hh = num / jnp.maximum(jnp.abs(den), jnp.exp(-m_t))
        hh = hh * _rms_scale(hh)
        og = _sigmoid(o_ref[:, h * Dh:(h + 1) * Dh])
        hm_ref[:, h * Dh:(h + 1) * Dh] = (hh * ng_ref[:, h * Dh:(h + 1) * Dh] * og).astype(BF16)

        g_r = a_tot - a_r + li_r
        g_c = a_tot - a_c + li_c
        m_loc = jnp.max(g_r, axis=-1, keepdims=True)
        kw = k * jnp.exp(g_c - m_loc)
        d_c = jnp.dot(kw.T.astype(BF16), vb, preferred_element_type=F32)
        d_n = jnp.sum(kw, axis=0, keepdims=True)
        m_new = jnp.maximum(a_tot + m0, m_loc)
        s_old = jnp.exp(a_tot + m0 - m_new)
        s_new = jnp.exp(m_loc - m_new)
        c_ref[h] = s_old * c0 + s_new * d_c
        n_ref[h] = s_old * n0 + s_new * d_n
        m_ref[h] = m_new


def _mlstm(z, gates, bias_row, conv_w, norm_g_row, batch, seq):
    T = z.shape[0]
    L, W = CHUNK, MLSTM_WIDTH
    nc = seq // L
    blk = lambda col: pl.BlockSpec((L, W), lambda b, c, col=col: (b * nc + c, col))
    return pl.pallas_call(
        _mlstm_kernel,
        grid=(batch, nc),
        in_specs=[
            blk(0), blk(1), blk(2), blk(3),
            pl.BlockSpec((L, LANES), lambda b, c: (b * nc + c, 0)),
            pl.BlockSpec((1, LANES), lambda b, c: (0, 0)),
            pl.BlockSpec((CONV_WIDTH, 2 * W), lambda b, c: (0, 0)),
            pl.BlockSpec((1, W), lambda b, c: (0, 0)),
        ],
        out_specs=pl.BlockSpec((L, W), lambda b, c: (b * nc + c, 0)),
        out_shape=jax.ShapeDtypeStruct((T, W), BF16),
        scratch_shapes=[
            pltpu.VMEM((L + 2 * SUBLANES, 2 * W), F32),
            pltpu.VMEM((MLSTM_HEADS, HEAD_DIM, HEAD_DIM), F32),
            pltpu.VMEM((MLSTM_HEADS, 1, HEAD_DIM), F32),
            pltpu.VMEM((MLSTM_HEADS, 1, 1), F32),
        ],
        compiler_params=_params("parallel", "arbitrary"),
        name="mlstm",
    )(z, z, z, z, gates, bias_row, conv_w, norm_g_row)


S5_BLOCK = 1024
S5_SEG = S5_BLOCK // SUBLANES
S5_TILES = S5_WIDTH // LANES
S5_TILE_STATE = 8 * S5_STATE


def _s5_kernel(u_ref, win_ref, wout_ref, wglu_ref, a_ref, aseg_ref, d_ref, ys_ref,
               up_ref, x_ref, xs_ref, yp_ref, carry_ref):
    nseg, seg, half = SUBLANES, S5_SEG, S5_TILE_STATE
    chains = range(u_ref.shape[0])

    @pl.when(pl.program_id(1) == 0)
    def _():
        carry_ref[...] = jnp.zeros_like(carry_ref)

    def cmul_add(ar, ai, sr, si, xr, xi):
        return ar * sr - ai * si + xr, ar * si + ai * sr + xi

    def x_rows(b, i):
        rows = slice(nseg * i, nseg * (i + 1))
        return x_ref[b, rows, 0:half], x_ref[b, rows, half:2 * half]

    for i in range(seg):
        for b in chains:
            up_ref[b, nseg * i:nseg * (i + 1), :] = u_ref[b, pl.ds(i, nseg, stride=seg), :]
    for b in chains:
        x_ref[b] = jnp.dot(up_ref[b].astype(BF16), win_ref[...], preferred_element_type=F32)
    ar = a_ref[:, 0:half]
    ai = a_ref[:, half:2 * half]

    fin = [(jnp.zeros((nseg, half), F32), jnp.zeros((nseg, half), F32)) for _ in chains]
    for i in range(seg):
        for b in chains:
            fin[b] = cmul_add(ar, ai, *fin[b], *x_rows(b, i))

    lr = aseg_ref[:, 0:half]
    lim = aseg_ref[:, half:2 * half]
    state = []
    for b in chains:
        cr = carry_ref[b, :, 0:half]
        ci = carry_ref[b, :, half:2 * half]
        crs, cis = [], []
        for r in range(nseg):
            crs.append(cr)
            cis.append(ci)
            cr, ci = cmul_add(lr, lim, cr, ci, fin[b][0][r:r + 1, :], fin[b][1][r:r + 1, :])
        carry_ref[b, :, 0:half] = cr
        carry_ref[b, :, half:2 * half] = ci
        state.append((jnp.concatenate(crs, axis=0), jnp.concatenate(cis, axis=0)))

    for i in range(seg):
        rows = slice(nseg * i, nseg * (i + 1))
        for b in chains:
            state[b] = cmul_add(ar, ai, *state[b], *x_rows(b, i))
            xs_ref[b, rows, 0:half] = state[b][0]
            xs_ref[b, rows, half:2 * half] = state[b][1]

    for b in chains:
        y = jnp.dot(xs_ref[b].astype(BF16), wout_ref[...], preferred_element_type=F32)
        y = y + d_ref[...] * up_ref[b]
        ab = jnp.dot(_gelu(y).astype(BF16), wglu_ref[...], preferred_element_type=F32)
        yp_ref[b] = ab[:, 0:LANES] * _sigmoid(ab[:, LANES:2 * LANES])
    for i in range(seg):
        for b in chains:
            ys_ref[b, pl.ds(i, nseg, stride=seg), :] = yp_ref[b, nseg * i:nseg * (i + 1), :]


def _s5(z, win, wout, wglu, a_rows, aseg_rows, d_row, batch, seq):
    T, N = z.shape
    blk = S5_BLOCK
    nb = seq // blk
    u_col = (4 * MLSTM_WIDTH) // LANES
    tile = lambda shape: pl.BlockSpec((None,) + shape, lambda j, c: (j,) + (0,) * len(shape))
    ys = pl.pallas_call(
        _s5_kernel,
        grid=(S5_TILES, nb),
        in_specs=[
            pl.BlockSpec((batch, blk, LANES), lambda j, c: (0, c, u_col + j)),
            tile(win.shape[1:]), tile(wout.shape[1:]), tile(wglu.shape[1:]),
            tile(a_rows.shape[1:]), tile(aseg_rows.shape[1:]),
            pl.BlockSpec((1, LANES), lambda j, c: (0, j)),
        ],
        out_specs=pl.BlockSpec((batch, blk, LANES), lambda j, c: (0, c, j)),
        out_shape=jax.ShapeDtypeStruct((batch, seq, S5_WIDTH), F32),
        scratch_shapes=[
            pltpu.VMEM((batch, blk, LANES), F32),
            pltpu.VMEM((batch, blk, 2 * S5_TILE_STATE), F32),
            pltpu.VMEM((batch, blk, 2 * S5_TILE_STATE), F32),
            pltpu.VMEM((batch, blk, LANES), F32),
            pltpu.VMEM((batch, 1, 2 * S5_TILE_STATE), F32),
        ],
        compiler_params=_params("parallel", "arbitrary"),
        name="s5",
    )(z.reshape(batch, seq, N), win, wout, wglu, a_rows, aseg_rows, d_row)
    return ys.reshape(T, S5_WIDTH)


def _s5_weights(lam_re, lam_im, log_dt, b_re, b_im, c_re, c_im, d, glu_w):
    G, P, Hg = S5_GROUPS, S5_STATE, S5_GROUP
    nt, gl = S5_TILES, G // S5_TILES
    dt = jnp.exp(log_dt.astype(F32))[:, None]
    mag = jnp.exp(lam_re * dt)
    ar = mag * jnp.cos(lam_im * dt)
    ai = mag * jnp.sin(lam_im * dt)
    den = lam_re * lam_re + lam_im * lam_im
    fr = ((ar - 1.0) * lam_re + ai * lam_im) / den
    fi = (ai * lam_re - (ar - 1.0) * lam_im) / den
    bbr = fr[..., None] * b_re - fi[..., None] * b_im
    bbi = fr[..., None] * b_im + fi[..., None] * b_re
    eye = jnp.eye(gl, dtype=F32)

    def tiles(v):
        return v.reshape((nt, gl) + v.shape[1:])

    bb = jnp.stack([tiles(bbr), tiles(bbi)], axis=2)
    win = jnp.einsum('tgrph,gk->tghrkp', bb, eye).reshape(nt, gl * Hg, 2 * gl * P)
    cc = jnp.stack([tiles(c_re), -tiles(c_im)], axis=2)
    wout = jnp.einsum('tgrhp,gk->trkpgh', cc, eye).reshape(nt, 2 * gl * P, gl * Hg)
    gw = tiles(glu_w).reshape(nt, gl, Hg, 2, Hg)
    wglu = jnp.einsum('tghrk,gq->tqhrgk', gw, eye).reshape(nt, gl * Hg, 2 * gl * Hg)

    def rows(re, im):
        return jnp.concatenate([tiles(re).reshape(nt, 1, gl * P), tiles(im).reshape(nt, 1, gl * P)], axis=-1)

    a_rows = jnp.broadcast_to(rows(ar, ai), (nt, SUBLANES, 2 * gl * P))
    pr, pi = ar, ai
    for _ in range(int(math.log2(S5_SEG))):
        pr, pi = pr * pr - pi * pi, 2.0 * pr * pi
    aseg_rows = rows(pr, pi)
    d_row = d.reshape(1, G * Hg).astype(F32)
    return win.astype(BF16), wout.astype(BF16), wglu.astype(BF16), a_rows, aseg_rows, d_row


def _outproj_kernel(hm_ref, ys_ref, w1_ref, w2_ref, x_ref, g_ref, h1_ref, xn_ref):
    mix = (jnp.dot(hm_ref[...], w1_ref[...].astype(BF16), preferred_element_type=F32)
           + jnp.dot(ys_ref[...].astype(BF16), w2_ref[...].astype(BF16), preferred_element_type=F32))
    h1 = x_ref[...] + mix
    h1_ref[...] = h1
    xn_ref[...] = (h1 * _rms_scale(h1) * g_ref[...]).astype(BF16)


def _outproj(hm, ys, w, x2d, g_row, tm=512):
    T, D = x2d.shape
    W = hm.shape[1]
    row = lambda w: pl.BlockSpec((tm, w), lambda i: (i, 0))
    half = lambda k: pl.BlockSpec((W, D), lambda i: (k, 0), pipeline_mode=pl.Buffered(1))
    return pl.pallas_call(
        _outproj_kernel,
        grid=(T // tm,),
        in_specs=[row(W), row(W), half(0), half(1), row(D), pl.BlockSpec((1, D), lambda i: (0, 0))],
        out_specs=[row(D), row(D)],
        out_shape=[jax.ShapeDtypeStruct((T, D), F32), jax.ShapeDtypeStruct((T, D), BF16)],
        compiler_params=_params("parallel"),
        name="outproj",
    )(hm, ys, w, w, x2d, g_row)


ROUTE_GROUP = 1024
ROUTE_CHUNK = LANES
GATE_PIECE = 4
GATE_LAG = 1
assert PEER_HEADS == SUBLANES


def _sort_pairs(n):
    pairs = []
    p = 1
    while p < n:
        k = p
        while k >= 1:
            for j in range(k % p, n - k, 2 * k):
                for i in range(min(k, n - j - k)):
                    if (i + j) // (2 * p) == (i + j + k) // (2 * p):
                        pairs.append((i + j, i + j + k))
            k //= 2
        p *= 2
    return pairs


def _first_second(a, b, ties=True):
    (av, ai), (bv, bi) = a, b
    if ties:
        tie = av == bv
        a_first = jnp.where(tie, bi, av) > jnp.where(tie, ai, bv)
    else:
        a_first = av > bv
    return ((jnp.maximum(av, bv), jnp.where(a_first, ai, bi)),
            (jnp.minimum(av, bv), jnp.where(a_first, bi, ai)))


def _sorted_group(items, ties=True):
    items = list(items)
    for i, j in _sort_pairs(len(items)):
        items[i], items[j] = _first_second(items[i], items[j], ties)
    return items


def _merge_top(a, b, ties=True):
    n = len(a)
    c = [_first_second(a[v], b[n - 1 - v], ties)[0] for v in range(n)]
    j = n // 2
    while j >= 1:
        for i in range(n):
            if (i & j) == 0:
                c[i], c[i + j] = _first_second(c[i], c[i + j], ties)
        j //= 2
    return c


def _top_sorted(groups, stage=lambda items: items, ties=True):
    groups = list(groups)
    while len(groups) > 1:
        groups = [_merge_top(stage(groups[i]), groups[i + 1], ties) for i in range(0, len(groups), 2)]
    return groups[0]


def _order_doubt(top, values):
    doubt = functools.reduce(lambda x, y: x + y,
                             [jnp.where(top[r][0] > top[r + 1][0], 0.0, 1.0) for r in range(len(top) - 1)])
    last = top[-1][0]
    count = functools.reduce(lambda x, y: x + y, [jnp.where(v >= last, 1.0, 0.0) for v in values])
    return doubt + jnp.where(count == float(len(top)), 0.0, 1.0)


def _product_top(top1, top2, stage=lambda items: items, ties=True):
    K = PEER_TOPK
    pairs = [(r1, r2) for r1 in range(K) for r2 in range(K) if (r1 + 1) * (r2 + 1) <= K]
    cands = [(top1[r1][0] + top2[r2][0], jnp.full_like(top1[0][0], float(r1 * K + r2))) for r1, r2 in pairs]
    pad = (jnp.full_like(top1[0][0], -jnp.inf), jnp.full_like(top1[0][0], float(K * K)))
    cands += [pad] * (-len(cands) % K)
    groups = [cands[:K]]
    for g in range(K, len(cands), K):
        groups.append(_sorted_group(stage(cands[g:g + K]), ties))
    best = _top_sorted(groups, stage, ties)
    doubt = None if ties else _order_doubt(best, [v for v, _ in cands[:len(pairs)]])
    top_s = [v for v, _ in best]
    e = [jnp.exp(v - top_s[0]) for v in top_s]
    z = functools.reduce(lambda x, y: x + y, e)
    gates, i1s, i2s = [], [], []
    for r in range(K):
        pos = best[r][1]
        r1 = jnp.floor(pos * np.float32(1.0 / K))
        r2 = pos - r1 * np.float32(K)
        i1, i2 = top1[0][1], top2[0][1]
        for a in range(1, K):
            i1 = jnp.where(r1 == float(a), top1[a][1], i1)
            i2 = jnp.where(r2 == float(a), top2[a][1], i2)
        gates.append(e[r] / z)
        i1s.append(i1)
        i2s.append(i2)
    return i1s, i2s, gates, doubt


def _route_gates_kernel(x_ref, wq_ref, sk_ref, g_ref, q_ref, sc_ref, slot_ref, tok_ref):
    K, n, half, H = PEER_TOPK, PEER_NKEYS, PEER_QDIM // 2, PEER_HEADS
    s = pl.program_id(0)
    n_chunks = pl.num_programs(0) - 1
    per_group = ROUTE_GROUP // ROUTE_CHUNK
    cur = s % 2

    @pl.when(s == 0)
    def _():
        slot_ref[...] = jnp.zeros_like(slot_ref)

    @pl.when(jnp.logical_and(s % per_group == 0, s < n_chunks))
    def _():
        q = jnp.dot(x_ref[...], wq_ref[...], preferred_element_type=F32)
        for hh in range(H):
            q_ref[hh] = q[:, hh * PEER_QDIM:(hh + 1) * PEER_QDIM].astype(BF16)

    for a in range(3):
        tok_ref[a] = slot_ref[1 - cur, a].T
    keys = lax.broadcasted_iota(jnp.int32, (n, n), 0).astype(F32)
    pieces = iter(range(ROUTE_CHUNK // GATE_PIECE))
    done = []

    def gate_piece():
        k = next(pieces, None)
        if k is None:
            return
        for t in range(k * GATE_PIECE, (k + 1) * GATE_PIECE):
            at = jnp.where(keys == tok_ref[0, t:t + 1, :], 1.0, 0.0).astype(BF16)
            bt = jnp.where(keys == tok_ref[1, t:t + 1, :], tok_ref[2, t:t + 1, :], 0.0).astype(BF16)
            g = lax.dot_general(at, bt, NT_DIMS, preferred_element_type=F32)
            g_ref[t] = g
        done.append(g[0:SUBLANES, :] * 0.0)

    def stage(items):
        gate_piece()
        if len(done) > GATE_LAG:
            (v, i), rest = items[0], list(items[1:])
            return [(v + done[len(done) - 1 - GATE_LAG], i)] + rest
        return items

    row0 = pl.multiple_of((s % per_group) * ROUTE_CHUNK, ROUTE_CHUNK)
    for p in range(2):
        for h in range(H):
            qc = q_ref[h, pl.ds(row0, ROUTE_CHUNK), p * half:(p + 1) * half]
            sc_ref[p, pl.ds(h, n, stride=H), :] = lax.dot_general(
                sk_ref[p], qc, NT_DIMS, preferred_element_type=F32)

    def route(ties, stage):
        tops, doubt = [], None
        for p in range(2):
            items = [(sc_ref[p, k * SUBLANES:(k + 1) * SUBLANES, :], jnp.full((SUBLANES, LANES), float(k), F32))
                     for k in range(n)]
            groups = [_sorted_group(stage(items[g:g + K]), ties) for g in range(0, n, K)]
            tops.append(_top_sorted(groups, stage, ties))
            if not ties:
                d = _order_doubt(tops[-1], [v for v, _ in items])
                doubt = d if doubt is None else doubt + d
        i1s, i2s, gates, d = _product_top(*tops, stage, ties)
        return (i1s, i2s, gates), (None if ties else doubt + d)

    def store(slots):
        for a, vals in enumerate(slots):
            for r in range(K):
                slot_ref[cur, a, pl.ds(r, H, stride=K), :] = vals[r]

    slots, doubt = route(False, stage)
    for _ in range(ROUTE_CHUNK // GATE_PIECE):
        gate_piece()
    store(slots)

    @pl.when(jnp.max(doubt) > 0.0)
    def _():
        store(route(True, lambda items: items)[0])


def _route_gates(xn2, wq, subkeys):
    T, D = xn2.shape
    n, slots = PEER_NKEYS, PEER_HEADS * PEER_TOPK
    n_chunks = T // ROUTE_CHUNK
    per_group = ROUTE_GROUP // ROUTE_CHUNK
    n_groups = T // ROUTE_GROUP
    once = pl.Buffered(1)
    return pl.pallas_call(
        _route_gates_kernel,
        grid=(n_chunks + 1,),
        in_specs=[
            pl.BlockSpec((ROUTE_GROUP, D), lambda s: (jnp.minimum(s // per_group, n_groups - 1), 0)),
            pl.BlockSpec(wq.shape, lambda s: (0, 0), pipeline_mode=once),
            pl.BlockSpec(subkeys.shape, lambda s: (0, 0, 0)),
        ],
        out_specs=pl.BlockSpec((ROUTE_CHUNK, n, n), lambda s: (jnp.maximum(s - 1, 0), 0, 0)),
        out_shape=jax.ShapeDtypeStruct((T, n, n), F32),
        scratch_shapes=[pltpu.VMEM((PEER_HEADS, ROUTE_GROUP, PEER_QDIM), BF16),
                        pltpu.VMEM((2, n * SUBLANES, LANES), F32),
                        pltpu.VMEM((2, 3, slots, ROUTE_CHUNK), F32),
                        pltpu.VMEM((3, ROUTE_CHUNK, slots), F32)],
        compiler_params=_params("arbitrary"),
        name="peer_route_gates",
    )(xn2, wq, subkeys)


EXPERT_G_ROWS = SUBLANES


def _experts_kernel(x_ref, u_ref, v_ref, g_ref, h1_ref, fg_ref, y_ref):
    e, k = pl.program_id(1), pl.program_id(2)
    first = jnp.logical_and(e == 0, k == 0)
    last = jnp.logical_and(e == pl.num_programs(1) - 1, k == pl.num_programs(2) - 1)

    @pl.when(first)
    def _():
        y_ref[...] = jnp.zeros_like(y_ref)

    u = u_ref[...].astype(BF16)
    v = v_ref[...].astype(BF16)
    act = _gelu(lax.dot_general(x_ref[...], u, NT_DIMS, preferred_element_type=F32))
    n = PEER_NKEYS
    tm, rows, _ = g_ref.shape
    nc = u_ref.shape[0] // n
    g_rows = g_ref.reshape(tm * rows, n)
    w = jnp.concatenate([(g_rows[pl.ds(k * nc + c, tm, stride=rows), :] * act[:, c * n:(c + 1) * n]).astype(BF16)
                         for c in range(nc)], axis=1)
    y_ref[...] += jnp.dot(w, v, preferred_element_type=F32)

    @pl.when(last)
    def _():
        h2 = h1_ref[...] + y_ref[...]
        y_ref[...] = h2 * _rms_scale(h2) * fg_ref[...]


def _experts(xn2, u_tab, v_tab, g3, h1, fg_row, tm=1024, te=512):
    T, D = xn2.shape
    E = u_tab.shape[0]
    n = PEER_NKEYS
    sub = EXPERT_G_ROWS * n // te
    once = pl.Buffered(1)
    return pl.pallas_call(
        _experts_kernel,
        grid=(T // tm, E // (te * sub), sub),
        in_specs=[
            pl.BlockSpec((tm, D), lambda i, e, k: (i, 0), pipeline_mode=once),
            pl.BlockSpec((te, D), lambda i, e, k: (e * sub + k, 0)),
            pl.BlockSpec((te, D), lambda i, e, k: (e * sub + k, 0)),
            pl.BlockSpec((tm, EXPERT_G_ROWS, n), lambda i, e, k: (i, e, 0)),
            pl.BlockSpec((tm, D), lambda i, e, k: (i, 0), pipeline_mode=once),
            pl.BlockSpec((1, D), lambda i, e, k: (0, 0)),
        ],
        out_specs=pl.BlockSpec((tm, D), lambda i, e, k: (i, 0), pipeline_mode=once),
        out_shape=jax.ShapeDtypeStruct((T, D), F32),
        compiler_params=_params("parallel", "arbitrary", "arbitrary"),
        name="peer_experts",
    )(xn2, u_tab, v_tab, g3, h1, fg_row)


def kernel(x, norm1_g, w_in, b_gates, conv_qk_w, mlstm_norm_g, s5_lambda_re, s5_lambda_im, s5_log_dt,
           s5_b_re, s5_b_im, s5_c_re, s5_c_im, s5_d, s5_glu_w, w_out, norm2_g, peer_wq, peer_subkeys,
           peer_u, peer_v, final_g):
    B, S, D = x.shape
    W, H = MLSTM_WIDTH, MLSTM_HEADS
    depth = norm1_g.shape[0]
    h = x.reshape(B * S, D)
    for l in range(depth):
        wl = jnp.swapaxes(w_in[l], 0, 1).astype(BF16)
        n_gate = 2 * H
        w_u = wl[4 * W + n_gate:]
        w_gates = jnp.pad(wl[4 * W:4 * W + n_gate], ((0, LANES - n_gate), (0, 0)))
        bias_row = jnp.pad(b_gates[l].astype(F32), (0, LANES - n_gate)).reshape(1, LANES)
        s5w = _s5_weights(s5_lambda_re[l].astype(F32), s5_lambda_im[l].astype(F32), s5_log_dt[l],
                          s5_b_re[l].astype(F32), s5_b_im[l].astype(F32), s5_c_re[l].astype(F32),
                          s5_c_im[l].astype(F32), s5_d[l], s5_glu_w[l].astype(F32))
        wo = w_out[l]

        z, gates = _inproj(h, norm1_g[l].reshape(1, D).astype(F32), wl, w_u, w_gates)
        hm = _mlstm(z, gates, bias_row, conv_qk_w[l].astype(F32),
                    mlstm_norm_g[l].reshape(1, W).astype(F32), B, S)
        ys = _s5(z, *s5w, B, S)
        h1, xn2 = _outproj(hm, ys, wo, h, norm2_g[l].reshape(1, D).astype(F32))

        g3 = _route_gates(xn2, peer_wq[l].astype(BF16), peer_subkeys[l].astype(BF16))
        assert l == depth - 1, "kernel is specialised to a single layer followed by the final norm"
        h = _experts(xn2, peer_u[l], peer_v[l], g3, h1,
                     final_g.reshape(1, D).astype(F32))
    return h.reshape(B, S, D)
```
